```python
import jax, jax.numpy as jnp
from jax import lax
import numpy as np

D_MODEL = 4096
BATCH = 4
SEQ = 4096
DEPTH = 4

CTX_LEN = 256
GRID_W = 64
HEAD_DIM = 128
N_HEADS_TOTAL = D_MODEL // HEAD_DIM
N_HEADS_NA = N_HEADS_TOTAL // 2
N_HEADS_WIN = N_HEADS_TOTAL - N_HEADS_NA
N_KV_WIN = max(1, N_HEADS_WIN // 4)
NA_ROWS = 8
NA_COLS = 16
WINDOW = 128
WIN_BLOCK = 128
ROPE_BASE = 10000.0
MLSTM_HEADS = 8
MLSTM_V = D_MODEL // MLSTM_HEADS
MLSTM_QK = MLSTM_V // 2
MLSTM_CHUNK = 64
N_EXPERTS = 16
EXPERT_FF = D_MODEL // 16
CAPACITY_FACTOR = 2
ADA_MODS = 6
NORM_EPS = 1e-6
NEG_INF = -1e30
N_EVEN = (DEPTH + 1) // 2
N_ODD = DEPTH // 2
NA_W = N_HEADS_NA * HEAD_DIM
WIN_Q_W = N_HEADS_WIN * HEAD_DIM
WIN_KV_W = N_KV_WIN * HEAD_DIM
AB_IN = 3 * NA_W + WIN_Q_W + 2 * WIN_KV_W
AB_OUT = NA_W + WIN_Q_W
MQK_W = MLSTM_HEADS * MLSTM_QK
MV_W = MLSTM_HEADS * MLSTM_V
N_GATES = 4 * MLSTM_HEADS
ML_IN = 2 * MQK_W + 2 * MV_W + N_GATES

kernel_name = 'hybrid_na_swa_mlstm_ecmoe_dit'


def rmsnorm(x, w):
    xf = x.astype(jnp.float32)
    y = xf * lax.rsqrt(jnp.mean(xf * xf, axis=-1, keepdims=True) + NORM_EPS)
    return (y * w.astype(jnp.float32)).astype(x.dtype)


def modulate(h, shift, scale):
    return h * (1 + scale) + shift


def to_heads(x, n_heads):
    b, t, w = x.shape
    return x.reshape(b, t, n_heads, w // n_heads).transpose(0, 2, 1, 3)


def from_heads(x):
    b, h, t, d = x.shape
    return x.transpose(0, 2, 1, 3).reshape(b, t, h * d)


def axial_rope(x):
    n, dh = x.shape[2], x.shape[3]
    half = dh // 2
    nf = half // 2
    t = jnp.arange(n)
    inv = ROPE_BASE ** (-jnp.arange(nf, dtype=jnp.float32) / nf)
    ang_r = (t // GRID_W).astype(jnp.float32)[:, None] * inv
    ang_c = (t % GRID_W).astype(jnp.float32)[:, None] * inv

    def rot(u, ang):
        u1, u2 = u[..., :nf], u[..., nf:]
        cos, sin = jnp.cos(ang), jnp.sin(ang)
        return jnp.concatenate([u1 * cos - u2 * sin, u2 * cos + u1 * sin], axis=-1)

    xf = x.astype(jnp.float32)
    return jnp.concatenate([rot(xf[..., :half], ang_r), rot(xf[..., half:], ang_c)], axis=-1).astype(x.dtype)


def dense_ctx_attn(q, k, v, sink):
    b, hq, lq, dh = q.shape
    hkv = k.shape[1]
    g = hq // hkv
    qg = q.reshape(b, hkv, g, lq, dh) * (dh ** -0.5)
    s = jnp.einsum('bkgqd,bkcd->bkgqc', qg, k).astype(jnp.float32)
    if sink is not None:
        sk = jnp.broadcast_to(sink.astype(jnp.float32).reshape(1, hkv, g, 1, 1), s.shape[:-1] + (1,))
        s = jnp.concatenate([s, sk], axis=-1)
    p = jax.nn.softmax(s, axis=-1)[..., :k.shape[2]]
    o = jnp.einsum('bkgqc,bkcd->bkgqd', p.astype(v.dtype), v)
    return o.reshape(b, hq, lq, dh)


def neighbourhood_attn(q, k, v, kc, vc, rpb):
    b, h, n, dh = q.shape
    rows = n // GRID_W
    kh, kw = min(NA_ROWS, rows), NA_COLS
    nk = kh * kw
    q = q * (dh ** -0.5)
    kg = k.reshape(b, h, rows, GRID_W, dh)
    vg = v.reshape(b, h, rows, GRID_W, dh)
    cols = jnp.arange(GRID_W)
    col_idx = jnp.clip(cols - kw // 2, 0, GRID_W - kw)[:, None] + jnp.arange(kw)[None, :]
    dc = col_idx - cols[:, None] + (NA_COLS - 1)
    row_ids = jnp.arange(rows)
    row_start = jnp.clip(row_ids - kh // 2, 0, rows - kh)

    def row_block(args):
        q_r, r, rs = args
        k_nb = lax.dynamic_slice_in_dim(kg, rs, kh, axis=2)[:, :, :, col_idx]
        v_nb = lax.dynamic_slice_in_dim(vg, rs, kh, axis=2)[:, :, :, col_idx]
        dr = rs + jnp.arange(kh) - r + (NA_ROWS - 1)
        bias = rpb[:, dr[:, None, None], dc[None]].transpose(0, 2, 1, 3)
        s_nb = (jnp.einsum('bhqd,bhiqjd->bhqij', q_r, k_nb) + bias[None]).reshape(b, h, GRID_W, nk)
        s_c = jnp.einsum('bhqd,bhcd->bhqc', q_r, kc)
        p = jax.nn.softmax(jnp.concatenate([s_nb, s_c], axis=-1).astype(jnp.float32), axis=-1).astype(v.dtype)
        p_nb = p[..., :nk].reshape(b, h, GRID_W, kh, kw)
        return jnp.einsum('bhqij,bhiqjd->bhqd', p_nb, v_nb) + jnp.einsum('bhqc,bhcd->bhqd', p[..., nk:], vc)

    q_rows = jnp.moveaxis(q.reshape(b, h, rows, GRID_W, dh), 2, 0)
    out = lax.map(row_block, (q_rows, row_ids, row_start))
    return jnp.moveaxis(out, 0, 2).reshape(b, h, n, dh)


def window_gqa_sink(q, k, v, kc, vc, sink):
    b, hq, n, dh = q.shape
    hkv = k.shape[1]
    g = hq // hkv
    nb = n // WIN_BLOCK
    span = WIN_BLOCK + 2 * WINDOW
    qb = (q * (dh ** -0.5)).reshape(b, hkv, g, nb, WIN_BLOCK, dh)
    pad = ((0, 0), (0, 0), (WINDOW, WINDOW), (0, 0))
    key_idx = jnp.arange(nb)[:, None] * WIN_BLOCK + jnp.arange(span)[None, :]
    kb = jnp.pad(k, pad)[:, :, key_idx]
    vb = jnp.pad(v, pad)[:, :, key_idx]
    qpos = jnp.arange(nb)[:, None] * WIN_BLOCK + jnp.arange(WIN_BLOCK)[None, :]
    kpos = (key_idx - WINDOW)[:, None, :]
    valid = (jnp.abs(qpos[:, :, None] - kpos) <= WINDOW) & (kpos >= 0) & (kpos < n)
    s_w = jnp.where(valid, jnp.einsum('bkgnqd,bknjd->bkgnqj', qb, kb).astype(jnp.float32), NEG_INF)
    s_c = jnp.einsum('bkgnqd,bkcd->bkgnqc', qb, kc).astype(jnp.float32)
    s_sink = jnp.broadcast_to(sink.astype(jnp.float32).reshape(1, hkv, g, 1, 1, 1), s_c.shape[:-1] + (1,))
    p = jax.nn.softmax(jnp.concatenate([s_w, s_c, s_sink], axis=-1), axis=-1).astype(v.dtype)
    lc = kc.shape[2]
    o = (jnp.einsum('bkgnqj,bknjd->bkgnqd', p[..., :span], vb)
         + jnp.einsum('bkgnqc,bkcd->bkgnqd', p[..., span:span + lc], vc))
    return o.reshape(b, hq, n, dh)


def mixer_ab(hc, hl, w_in, w_out, rpb, sink, need_ctx):
    splits = [NA_W, 2 * NA_W, 3 * NA_W, 3 * NA_W + WIN_Q_W, 3 * NA_W + WIN_Q_W + WIN_KV_W]

    def project(h):
        qa, ka, va, qb, kb, vb = jnp.split(h @ w_in, splits, axis=-1)
        return (to_heads(qa, N_HEADS_NA), to_heads(ka, N_HEADS_NA), to_heads(va, N_HEADS_NA),
                to_heads(qb, N_HEADS_WIN), to_heads(kb, N_KV_WIN), to_heads(vb, N_KV_WIN))

    qa_c, ka_c, va_c, qb_c, kb_c, vb_c = project(hc)
    qa_l, ka_l, va_l, qb_l, kb_l, vb_l = project(hl)
    out_a = neighbourhood_attn(qa_l, ka_l, va_l, ka_c, va_c, rpb)
    out_b = window_gqa_sink(axial_rope(qb_l), axial_rope(kb_l), vb_l, kb_c, vb_c, sink)
    yl = jnp.concatenate([from_heads(out_a), from_heads(out_b)], axis=-1) @ w_out
    yc = None
    if need_ctx:
        oa_c = dense_ctx_attn(qa_c, ka_c, va_c, None)
        ob_c = dense_ctx_attn(qb_c, kb_c, vb_c, sink)
        yc = jnp.concatenate([from_heads(oa_c), from_heads(ob_c)], axis=-1) @ w_out
    return yc, yl


def mlstm_scan(q, k, v, i_pre, f_pre, state, return_h):
    b, h, t, dv = v.shape
    nc = t // MLSTM_CHUNK

    def chunks(a):
        return jnp.moveaxis(a.reshape(a.shape[:2] + (nc, MLSTM_CHUNK) + a.shape[3:]), 2, 0)

    bcum = jnp.cumsum(chunks(jax.nn.log_sigmoid(f_pre)), axis=-1)
    tril = jnp.tril(jnp.ones((MLSTM_CHUNK, MLSTM_CHUNK), dtype=bool))

    def step(carry, xs):
        cmat, nvec, m = carry
        q_c, k_c, v_c, i_c, b_c = xs
        qf, kf, vf = q_c.astype(jnp.float32), k_c.astype(jnp.float32), v_c.astype(jnp.float32)
        g = b_c[..., -1]
        a = g[..., None] - b_c + i_c
        m_new = jnp.maximum(g + m, jnp.max(a, axis=-1))
        wk = jnp.exp(a - m_new[..., None])
        decay = jnp.exp(g + m - m_new)
        c_new = decay[..., None, None] * cmat + jnp.einsum('bhsd,bhse->bhde', kf * wk[..., None], vf)
        n_new = decay[..., None] * nvec + jnp.einsum('bhsd,bhs->bhd', kf, wk)
        if not return_h:
            return (c_new, n_new, m_new), None
        dmat = jnp.where(tril, b_c[..., :, None] - b_c[..., None, :] + i_c[..., None, :], -jnp.inf)
        m_inter = b_c + m[..., None]
        m_t = jnp.maximum(m_inter, jnp.max(dmat, axis=-1))
        w_inter = jnp.exp(m_inter - m_t)
        pw = jnp.exp(dmat - m_t[..., None]) * jnp.einsum('bhtd,bhsd->bhts', qf, kf)
        num = w_inter[..., None] * jnp.einsum('bhtd,bhde->bhte', qf, cmat) + jnp.einsum('bhts,bhse->bhte', pw, vf)
        den = w_inter * jnp.einsum('bhtd,bhd->bht', qf, nvec) + jnp.sum(pw, axis=-1)
        hc = num / jnp.maximum(jnp.abs(den), jnp.exp(-m_t))[..., None]
        return (c_new, n_new, m_new), hc

    final, hs = lax.scan(step, state, (chunks(q), chunks(k), chunks(v), chunks(i_pre), bcum))
    hout = None
    if return_h:
        hout = jnp.moveaxis(hs, 0, 2).reshape(b, h, t, dv).astype(v.dtype)
    return hout, final


def mlstm_out(hsum, o, norm_w, w_out):
    b, nh, t, dv = hsum.shape
    hf = jnp.swapaxes(hsum, 1, 2).astype(jnp.float32)
    hf = hf * lax.rsqrt(jnp.mean(hf * hf, axis=-1, keepdims=True) + NORM_EPS)
    hn = (hf.reshape(b, t, nh * dv) * norm_w.astype(jnp.float32)).astype(o.dtype)
    return (hn * jax.nn.sigmoid(o)) @ w_out


def mixer_c(hc, hl, w_in, b_gates, norm_w, w_out, need_ctx):
    splits = [MQK_W, 2 * MQK_W, 2 * MQK_W + MV_W, 2 * MQK_W + 2 * MV_W]

    def project(h):
        q, k, v, o, gt = jnp.split(h @ w_in, splits, axis=-1)
        b, t, _ = gt.shape
        gt = (gt + b_gates).astype(jnp.float32).reshape(b, t, 4, MLSTM_HEADS).transpose(2, 0, 3, 1)
        return (to_heads(q, MLSTM_HEADS) * (MLSTM_QK ** -0.5), to_heads(k, MLSTM_HEADS),
                to_heads(v, MLSTM_HEADS), o, gt)

    qc, kc, vc, oc, gc = project(hc)
    ql, kl, vl, ol, gl = project(hl)
    b = hl.shape[0]
    state0 = (jnp.zeros((b, MLSTM_HEADS, MLSTM_QK, MLSTM_V), jnp.float32),
              jnp.zeros((b, MLSTM_HEADS, MLSTM_QK), jnp.float32),
              jnp.zeros((b, MLSTM_HEADS), jnp.float32))
    flip = lambda a: jnp.flip(a, axis=2)
    hc_f, st_f = mlstm_scan(qc, kc, vc, gc[0], gc[1], state0, need_ctx)
    hl_f, _ = mlstm_scan(ql, kl, vl, gl[0], gl[1], st_f, True)
    hc_b, st_b = mlstm_scan(flip(qc), flip(kc), flip(vc), flip(gc[2]), flip(gc[3]), state0, need_ctx)
    hl_b, _ = mlstm_scan(flip(ql), flip(kl), flip(vl), flip(gl[2]), flip(gl[3]), st_b, True)
    yl = mlstm_out(hl_f + flip(hl_b), ol, norm_w, w_out)
    yc = None
    if need_ctx:
        yc = mlstm_out(hc_f + flip(hc_b), oc, norm_w, w_out)
    return yc, yl


def ec_moe(h, w_router, w1, w3, w2):
    b, t, d = h.shape
    cap = (CAPACITY_FACTOR * t) // N_EXPERTS
    aff = jax.nn.softmax((h @ w_router).astype(jnp.float32), axis=-1)
    gate, idx = lax.top_k(jnp.swapaxes(aff, 1, 2), cap)
    xs = jax.vmap(lambda hb, ib: hb[ib])(h, idx)
    a = jnp.einsum('becd,edf->becf', xs, w1)
    u = jnp.einsum('becd,edf->becf', xs, w3)
    y = jnp.einsum('becf,efd->becd', jax.nn.silu(a) * u, w2)
    y = (y * gate[..., None]).astype(h.dtype)
    return jax.vmap(lambda ib, yb: jnp.zeros((t, d), h.dtype).at[ib.reshape(-1)].add(yb.reshape(-1, d)))(idx, y)


def setup_inputs(seed: int = 0) -> dict:
    key = jax.random.key(seed)
    ks = jax.random.split(key, 24)

    def nrm(k, shape, scale):
        return jax.random.normal(k, shape, jnp.float32) * scale

    gate_base = jnp.repeat(jnp.array([0.0, 3.0, 0.0, 3.0], jnp.float32), MLSTM_HEADS)
    return {
        'x': nrm(ks[0], (BATCH, SEQ, D_MODEL), 1.0),
        'c': nrm(ks[1], (BATCH, D_MODEL), 1.0),
        'ctx': nrm(ks[2], (BATCH, CTX_LEN, D_MODEL), 1.0),
        'c_ctx': nrm(ks[3], (D_MODEL,), 1.0),
        'ada_w': nrm(ks[4], (DEPTH, D_MODEL, ADA_MODS * D_MODEL), 0.3 * D_MODEL ** -0.5),
        'ada_b': nrm(ks[5], (DEPTH, ADA_MODS * D_MODEL), 0.02),
        'norm1_w': 1.0 + nrm(ks[6], (DEPTH, D_MODEL), 0.1),
        'norm2_w': 1.0 + nrm(ks[7], (DEPTH, D_MODEL), 0.1),
        'ab_w_in': nrm(ks[8], (N_EVEN, D_MODEL, AB_IN), D_MODEL ** -0.5),
        'ab_w_out': nrm(ks[9], (N_EVEN, AB_OUT, D_MODEL), AB_OUT ** -0.5),
        'na_rpb': nrm(ks[10], (N_EVEN, N_HEADS_NA, 2 * NA_ROWS - 1, 2 * NA_COLS - 1), 0.5),
        'win_sink': nrm(ks[11], (N_EVEN, N_HEADS_WIN), 1.0),
        'ml_w_in': nrm(ks[12], (N_ODD, D_MODEL, ML_IN), D_MODEL ** -0.5),
        'ml_b_gates': gate_base + nrm(ks[13], (N_ODD, N_GATES), 0.5),
        'ml_norm_w': 1.0 + nrm(ks[14], (N_ODD, MV_W), 0.1),
        'ml_w_out': nrm(ks[15], (N_ODD, MV_W, D_MODEL), MV_W ** -0.5),
        'moe_router': nrm(ks[16], (DEPTH, D_MODEL, N_EXPERTS), D_MODEL ** -0.5),
        'moe_w1': nrm(ks[17], (DEPTH, N_EXPERTS, D_MODEL, EXPERT_FF), D_MODEL ** -0.5),
        'moe_w3': nrm(ks[18], (DEPTH, N_EXPERTS, D_MODEL, EXPERT_FF), D_MODEL ** -0.5),
        'moe_w2': nrm(ks[19], (DEPTH, N_EXPERTS, EXPERT_FF, D_MODEL), EXPERT_FF ** -0.5),
        'final_norm_w': 1.0 + nrm(ks[20], (D_MODEL,), 0.1),
    }


def reference(x, c, ctx, c_ctx, ada_w, ada_b, norm1_w, norm2_w, ab_w_in, ab_w_out, na_rpb, win_sink,
              ml_w_in, ml_b_gates, ml_norm_w, ml_w_out, moe_router, moe_w1, moe_w3, moe_w2, final_norm_w):
    xl, xc = x, ctx
    sc = jax.nn.silu(c)
    scc = jax.nn.silu(c_ctx)
    for l in range(DEPTH):
        need_ctx = l < DEPTH - 1
        mod_l = (sc @ ada_w[l] + ada_b[l])[:, None, :]
        mod_c = scc @ ada_w[l] + ada_b[l]
        sh1, sc1, g1, sh2, sc2, g2 = jnp.split(mod_l, ADA_MODS, axis=-1)
        ch1, cs1, cg1, ch2, cs2, cg2 = jnp.split(mod_c, ADA_MODS, axis=-1)
        hl = modulate(rmsnorm(xl, norm1_w[l]), sh1, sc1)
        hc = modulate(rmsnorm(xc, norm1_w[l]), ch1, cs1)
        if l % 2 == 0:
            e = l // 2
            yc, yl = mixer_ab(hc, hl, ab_w_in[e], ab_w_out[e], na_rpb[e], win_sink[e], need_ctx)
        else:
            o = l // 2
            yc, yl = mixer_c(hc, hl, ml_w_in[o], ml_b_gates[o], ml_norm_w[o], ml_w_out[o], need_ctx)
        xl = xl + g1 * yl
        hl2 = modulate(rmsnorm(xl, norm2_w[l]), sh2, sc2)
        xl = xl + g2 * ec_moe(hl2, moe_router[l], moe_w1[l], moe_w3[l], moe_w2[l])
        if need_ctx:
            xc = xc + cg1 * yc
            hc2 = modulate(rmsnorm(xc, norm2_w[l]), ch2, cs2)
            xc = xc + cg2 * ec_moe(hc2, moe_router[l], moe_w1[l], moe_w3[l], moe_w2[l])
    return rmsnorm(xl, final_norm_w)
```

```python
import functools

import numpy as np
import jax
import jax.numpy as jnp
from jax import lax
from jax.experimental import pallas as pl
from jax.experimental.pallas import tpu as pltpu

GRID_W = 64
HEAD_DIM = 128
NA_ROWS = 8
NA_COLS = 16
WINDOW = 128
WIN_BLOCK = 128
ROPE_BASE = 10000.0
MLSTM_HEADS = 8
MLSTM_KERNEL_CHUNK = 256
CAPACITY_FACTOR = 2
ADA_MODS = 6
NORM_EPS = 1e-6
NEG_INF = -1e30
LANES = 128
V7X_VMEM_LIMIT_BYTES = 56 * 1024 * 1024

F32 = jnp.float32
BF16 = jnp.bfloat16
_NT = (((1,), (1,)), ((), ()))
_TN = (((0,), (0,)), ((), ()))


def _pick(n, cands):
    for c in cands:
        if n % c == 0:
            return c
    raise ValueError(f"no tile in {cands} divides {n}")


def _params(sem):
    return pltpu.CompilerParams(dimension_semantics=sem, vmem_limit_bytes=V7X_VMEM_LIMIT_BYTES)


def _ada_kernel(cc_ref, w_ref, b_ref, o_ref):
    x = cc_ref[...]
    s = x * jax.nn.sigmoid(x)
    o_ref[...] = jnp.dot(s.astype(BF16), w_ref[...].astype(BF16), preferred_element_type=F32) + b_ref[...]


def _ada_mods(cc, ada_w, ada_b):
    depth, d, n = ada_w.shape
    tn = _pick(n, (512, 256, 128))
    return pl.pallas_call(
        _ada_kernel,
        grid=(depth, n // tn),
        in_specs=[pl.BlockSpec((8, d), lambda l, j: (0, 0)),
                  pl.BlockSpec((None, d, tn), lambda l, j: (l, 0, j)),
                  pl.BlockSpec((None, 1, tn), lambda l, j: (l, 0, j))],
        out_specs=pl.BlockSpec((None, 8, tn), lambda l, j: (l, 0, j)),
        out_shape=jax.ShapeDtypeStruct((depth, 8, n), F32),
        compiler_params=_params(("arbitrary", "arbitrary")),
        name="ada_mods",
    )(cc, ada_w, ada_b.reshape(depth, 1, n))


def _norm_body(x_ref, w_ref, sh_ref, sc_ref):
    x = x_ref[...]
    y = x * lax.rsqrt(jnp.mean(x * x, axis=-1, keepdims=True) + NORM_EPS) * w_ref[...]
    return y * (1.0 + sc_ref[...]) + sh_ref[...]


def _norm_kernel(x_ref, w_ref, sh_ref, sc_ref, o_ref):
    o_ref[...] = _norm_body(x_ref, w_ref, sh_ref, sc_ref).astype(o_ref.dtype)


def _norm_router_kernel(x_ref, w_ref, sh_ref, sc_ref, wr_ref, o_ref, aff_ref):
    hb = _norm_body(x_ref, w_ref, sh_ref, sc_ref).astype(BF16)
    o_ref[...] = hb
    logits = lax.dot_general(wr_ref[...], hb, _NT, preferred_element_type=F32)
    e = jnp.exp(logits - jnp.max(logits, axis=0, keepdims=True))
    aff_ref[...] = e / jnp.sum(e, axis=0, keepdims=True)


def _norm_mod(x, w, shift, scale, n_tok, out_dtype=BF16, router_t=None):
    m, d = x.shape
    tm = _pick(n_tok, (256, 128))
    tiles = n_tok // tm
    per_sample = shift.shape[0] > 1
    mod_idx = (lambda i: (i // tiles, 0, 0)) if per_sample else (lambda i: (0, 0, 0))
    in_specs = [pl.BlockSpec((tm, d), lambda i: (i, 0)),
                pl.BlockSpec((1, d), lambda i: (0, 0)),
                pl.BlockSpec((None, 1, d), mod_idx),
                pl.BlockSpec((None, 1, d), mod_idx)]
    args = [x, w.reshape(1, d), shift, scale]
    o_spec = pl.BlockSpec((tm, d), lambda i: (i, 0))
    if router_t is None:
        return pl.pallas_call(
            _norm_kernel, grid=(m // tm,), in_specs=in_specs, out_specs=o_spec,
            out_shape=jax.ShapeDtypeStruct((m, d), out_dtype),
            compiler_params=_params(("arbitrary",)), name="norm_mod")(*args)
    n_exp = router_t.shape[0]
    return pl.pallas_call(
        _norm_router_kernel, grid=(m // tm,),
        in_specs=in_specs + [pl.BlockSpec((n_exp, d), lambda i: (0, 0))],
        out_specs=[o_spec, pl.BlockSpec((None, n_exp, tm), lambda i: (i // tiles, 0, i % tiles))],
        out_shape=[jax.ShapeDtypeStruct((m, d), BF16),
                   jax.ShapeDtypeStruct((m // n_tok, n_exp, n_tok), F32)],
        compiler_params=_params(("arbitrary",)), name="norm_mod_router")(*args, router_t)


def _mm_kernel(a_ref, w_ref, o_ref):
    o_ref[...] = jnp.dot(a_ref[...], w_ref[...], preferred_element_type=F32).astype(o_ref.dtype)


def _mm_bias_kernel(a_ref, w_ref, b_ref, o_ref):
    o_ref[...] = jnp.dot(a_ref[...], w_ref[...], preferred_element_type=F32) + b_ref[...]


def _mm_res_kernel(a_ref, w_ref, r_ref, g_ref, o_ref):
    o_ref[...] = r_ref[...] + g_ref[...] * jnp.dot(a_ref[...], w_ref[...], preferred_element_type=F32)


def _mm_tiles(m, n):
    return _pick(m, (1024, 512, 256, 128)), _pick(n, (512, 256, 128))


def _matmul(a, w, out_dtype=BF16, bias=None):
    m, k = a.shape
    n = w.shape[1]
    tm, tn = _mm_tiles(m, n)
    in_specs = [pl.BlockSpec((tm, k), lambda i, j: (i, 0)), pl.BlockSpec((k, tn), lambda i, j: (0, j))]
    args = [a, w]
    body = _mm_kernel
    if bias is not None:
        in_specs.append(pl.BlockSpec((1, tn), lambda i, j: (0, j)))
        args.append(bias.reshape(1, n))
        body = _mm_bias_kernel
    return pl.pallas_call(
        body, grid=(m // tm, n // tn), in_specs=in_specs,
        out_specs=pl.BlockSpec((tm, tn), lambda i, j: (i, j)),
        out_shape=jax.ShapeDtypeStruct((m, n), out_dtype),
        compiler_params=_params(("arbitrary", "arbitrary")), name="matmul")(*args)


def _matmul_residual(a, w, res, gate, n_tok):
    m, k = a.shape
    n = w.shape[1]
    tm = _pick(n_tok, (1024, 512, 256, 128))
    tn = _pick(n, (512, 256, 128))
    tiles = n_tok // tm
    g_idx = (lambda i, j: (i // tiles, 0, j)) if gate.shape[0] > 1 else (lambda i, j: (0, 0, j))
    return pl.pallas_call(
        _mm_res_kernel, grid=(m // tm, n // tn),
        in_specs=[pl.BlockSpec((tm, k), lambda i, j: (i, 0)),
                  pl.BlockSpec((k, tn), lambda i, j: (0, j)),
                  pl.BlockSpec((tm, tn), lambda i, j: (i, j)),
                  pl.BlockSpec((None, 1, tn), g_idx)],
        out_specs=pl.BlockSpec((tm, tn), lambda i, j: (i, j)),
        out_shape=jax.ShapeDtypeStruct((m, n), F32),
        input_output_aliases={2: 0},
        compiler_params=_params(("arbitrary", "arbitrary")), name="matmul_residual")(a, w, res, gate)


def _na_bias_table(rpb, rows):
    kh = min(NA_ROWS, rows)
    cols = np.arange(GRID_W)
    cstart = np.clip(cols - NA_COLS // 2, 0, GRID_W - NA_COLS)
    valid = (cols[None, :] >= cstart[:, None]) & (cols[None, :] < cstart[:, None] + NA_COLS)
    dc = np.clip(cols[None, :] - cols[:, None] + NA_COLS - 1, 0, 2 * NA_COLS - 2)
    dr = np.arange(kh)[None, :] - np.arange(kh)[:, None] + NA_ROWS - 1
    b = rpb[:, dr[:, None, :, None], dc[None, :, None, :]]
    b = jnp.where(valid[None, None, :, None, :], b.astype(F32), NEG_INF)
    return b.reshape(rpb.shape[0], kh, GRID_W, kh * GRID_W)


def _na_kernel(q_ref, k_ref, v_ref, kc_ref, vc_ref, bias_ref, o_ref, *, rb, rows, kh, scale):
    r0 = pl.program_id(2) * rb
    kc = kc_ref[...]
    vc = vc_ref[...]
    for i in range(rb):
        r = r0 + i
        rs = jnp.clip(r - kh // 2, 0, rows - kh)
        start = pl.multiple_of(rs * GRID_W, GRID_W)
        q = q_ref[i * GRID_W:(i + 1) * GRID_W, :]
        kw = k_ref[pl.ds(start, kh * GRID_W), :]
        vw = v_ref[pl.ds(start, kh * GRID_W), :]
        s = lax.dot_general(q, kw, _NT, preferred_element_type=F32) * scale + bias_ref[r - rs]
        sc = lax.dot_general(q, kc, _NT, preferred_element_type=F32) * scale
        mx = jnp.maximum(jnp.max(s, axis=-1, keepdims=True), jnp.max(sc, axis=-1, keepdims=True))
        p = jnp.exp(s - mx)
        pc = jnp.exp(sc - mx)
        den = jnp.sum(p, axis=-1, keepdims=True) + jnp.sum(pc, axis=-1, keepdims=True)
        o = (jnp.dot(p.astype(BF16), vw, preferred_element_type=F32)
             + jnp.dot(pc.astype(BF16), vc, preferred_element_type=F32))
        o_ref[i * GRID_W:(i + 1) * GRID_W, :] = (o / den).astype(o_ref.dtype)


def _na_attention(qkv_l, qkv_c, bias, batch, n, lc, n_heads, out_width):
    rows = n // GRID_W
    kh = min(NA_ROWS, rows)
    rb = _pick(rows, (8, 4, 2, 1))
    nrb = rows // rb
    hd = HEAD_DIM
    kern = functools.partial(_na_kernel, rb=rb, rows=rows, kh=kh, scale=hd ** -0.5)
    return pl.pallas_call(
        kern, grid=(batch, n_heads, nrb),
        in_specs=[pl.BlockSpec((rb * GRID_W, hd), lambda b, h, r: (b * nrb + r, h)),
                  pl.BlockSpec((n, hd), lambda b, h, r: (b, n_heads + h)),
                  pl.BlockSpec((n, hd), lambda b, h, r: (b, 2 * n_heads + h)),
                  pl.BlockSpec((lc, hd), lambda b, h, r: (b, n_heads + h)),
                  pl.BlockSpec((lc, hd), lambda b, h, r: (b, 2 * n_heads + h)),
                  pl.BlockSpec((None, kh, GRID_W, kh * GRID_W), lambda b, h, r: (h, 0, 0, 0))],
        out_specs=pl.BlockSpec((rb * GRID_W, hd), lambda b, h, r: (b * nrb + r, h)),
        out_shape=jax.ShapeDtypeStruct((batch * n, out_width), BF16),
        compiler_params=_params(("arbitrary", "arbitrary", "arbitrary")), name="na_attention",
    )(qkv_l, qkv_l, qkv_l, qkv_c, qkv_c, bias)


def _rope_tables(n):
    nf = HEAD_DIM // 4
    t = jnp.arange(n)
    inv = ROPE_BASE ** (-jnp.arange(nf, dtype=F32) / nf)
    ang_r = (t // GRID_W).astype(F32)[:, None] * inv
    ang_c = (t % GRID_W).astype(F32)[:, None] * inv
    cos = jnp.concatenate([jnp.cos(ang_r), jnp.cos(ang_r), jnp.cos(ang_c), jnp.cos(ang_c)], axis=-1)
    sin = jnp.concatenate([-jnp.sin(ang_r), jnp.sin(ang_r), -jnp.sin(ang_c), jnp.sin(ang_c)], axis=-1)
    return cos, sin


def _rope(x, cos, sin_signed):
    nf = HEAD_DIM // 4
    lane = lax.broadcasted_iota(jnp.int32, x.shape, 1)
    first = (lane % (2 * nf)) < nf
    swapped = jnp.where(first, pltpu.roll(x, HEAD_DIM - nf, 1), pltpu.roll(x, nf, 1))
    return x * cos + swapped * sin_signed


def _sink_column(sink_ref, base, g, rows_per_head):
    return jnp.concatenate([jnp.full((rows_per_head, 1), sink_ref[base + j], F32) for j in range(g)], axis=0)


def _win_kernel(sink_ref, q_ref, k_ref, v_ref, kc_ref, vc_ref, cq_ref, sq_ref, ck_ref, sk_ref, prev_ref, o_ref,
                *, g, n, scale):
    del prev_ref
    hd = HEAD_DIM
    kvh = pl.program_id(1)
    blk = pl.program_id(2)
    span = WIN_BLOCK + 2 * WINDOW
    s0 = pl.multiple_of(jnp.clip(blk * WIN_BLOCK - WINDOW, 0, n - span), LANES)
    cq = cq_ref[...]
    sq = sq_ref[...]
    q = q_ref[...]
    qs = jnp.concatenate([_rope(q[:, j * hd:(j + 1) * hd].astype(F32), cq, sq) for j in range(g)],
                         axis=0).astype(BF16)
    kw = _rope(k_ref[pl.ds(s0, span), :].astype(F32), ck_ref[pl.ds(s0, span), :],
               sk_ref[pl.ds(s0, span), :]).astype(BF16)
    vw = v_ref[pl.ds(s0, span), :]
    s = lax.dot_general(qs, kw, _NT, preferred_element_type=F32) * scale
    qpos = blk * WIN_BLOCK + lax.broadcasted_iota(jnp.int32, (g * WIN_BLOCK, 1), 0) % WIN_BLOCK
    kpos = s0 + lax.broadcasted_iota(jnp.int32, (1, span), 1)
    s = jnp.where(jnp.abs(qpos - kpos) <= WINDOW, s, NEG_INF)
    sc = lax.dot_general(qs, kc_ref[...], _NT, preferred_element_type=F32) * scale
    sink = _sink_column(sink_ref, kvh * g, g, WIN_BLOCK)
    mx = jnp.maximum(jnp.maximum(jnp.max(s, axis=-1, keepdims=True), jnp.max(sc, axis=-1, keepdims=True)), sink)
    p = jnp.exp(s - mx)
    pc = jnp.exp(sc - mx)
    den = jnp.sum(p, axis=-1, keepdims=True) + jnp.sum(pc, axis=-1, keepdims=True) + jnp.exp(sink - mx)
    o = (jnp.dot(p.astype(BF16), vw, preferred_element_type=F32)
         + jnp.dot(pc.astype(BF16), vc_ref[...], preferred_element_type=F32)) / den
    o_ref[...] = jnp.concatenate([o[j * WIN_BLOCK:(j + 1) * WIN_BLOCK] for j in range(g)],
                                 axis=1).astype(o_ref.dtype)


def _win_attention(attn, sink, qkv_l, qkv_c, cos, sin, batch, n, lc, n_na, n_q, n_kv):
    hd = HEAD_DIM
    g = n_q // n_kv
    nb = n // WIN_BLOCK
    q_blk0 = (3 * n_na * hd) // (g * hd)
    k_blk0 = 3 * n_na + n_q
    o_blk0 = (n_na * hd) // (g * hd)
    assert (3 * n_na) % g == 0 and n_na % g == 0
    kern = functools.partial(_win_kernel, g=g, n=n, scale=hd ** -0.5)
    full = lambda b, kv, i: (0, 0)
    return pl.pallas_call(
        kern, grid=(batch, n_kv, nb),
        in_specs=[pl.BlockSpec(memory_space=pltpu.SMEM),
                  pl.BlockSpec((WIN_BLOCK, g * hd), lambda b, kv, i: (b * nb + i, q_blk0 + kv)),
                  pl.BlockSpec((n, hd), lambda b, kv, i: (b, k_blk0 + kv)),
                  pl.BlockSpec((n, hd), lambda b, kv, i: (b, k_blk0 + n_kv + kv)),
                  pl.BlockSpec((lc, hd), lambda b, kv, i: (b, k_blk0 + kv)),
                  pl.BlockSpec((lc, hd), lambda b, kv, i: (b, k_blk0 + n_kv + kv)),
                  pl.BlockSpec((WIN_BLOCK, hd), lambda b, kv, i: (i, 0)),
                  pl.BlockSpec((WIN_BLOCK, hd), lambda b, kv, i: (i, 0)),
                  pl.BlockSpec((n, hd), full),
                  pl.BlockSpec((n, hd), full),
                  pl.BlockSpec(memory_space=pl.ANY)],
        out_specs=pl.BlockSpec((WIN_BLOCK, g * hd), lambda b, kv, i: (b * nb + i, o_blk0 + kv)),
        out_shape=jax.ShapeDtypeStruct(attn.shape, attn.dtype),
        input_output_aliases={10: 0},
        compiler_params=_params(("arbitrary", "arbitrary", "arbitrary")), name="window_attention",
    )(sink, qkv_l, qkv_l, qkv_l, qkv_c, qkv_c, cos, sin, cos, sin, attn)


def _ctx_attn_kernel(sink_ref, q_ref, k_ref, v_ref, prev_ref, o_ref, *, g, lc, use_sink, scale):
    del prev_ref
    hd = HEAD_DIM
    q = q_ref[...]
    qs = jnp.concatenate([q[:, j * hd:(j + 1) * hd] for j in range(g)], axis=0)
    s = lax.dot_general(qs, k_ref[...], _NT, preferred_element_type=F32) * scale
    mx = jnp.max(s, axis=-1, keepdims=True)
    if use_sink:
        sink = _sink_column(sink_ref, pl.program_id(1) * g, g, lc)
        mx = jnp.maximum(mx, sink)
    p = jnp.exp(s - mx)
    den = jnp.sum(p, axis=-1, keepdims=True)
    if use_sink:
        den = den + jnp.exp(sink - mx)
    o = jnp.dot(p.astype(BF16), v_ref[...], preferred_element_type=F32) / den
    o_ref[...] = jnp.concatenate([o[j * lc:(j + 1) * lc] for j in range(g)], axis=1).astype(o_ref.dtype)


def _ctx_attention(prev, sink, qkv_c, batch, lc, n_groups, g, q_blk0, k_blk0, v_blk0, o_blk0, use_sink):
    hd = HEAD_DIM
    kern = functools.partial(_ctx_attn_kernel, g=g, lc=lc, use_sink=use_sink, scale=hd ** -0.5)
    return pl.pallas_call(
        kern, grid=(batch, n_groups),
        in_specs=[pl.BlockSpec(memory_space=pltpu.SMEM),
                  pl.BlockSpec((lc, g * hd), lambda b, h: (b, q_blk0 + h)),
                  pl.BlockSpec((lc, hd), lambda b, h: (b, k_blk0 + h)),
                  pl.BlockSpec((lc, hd), lambda b, h: (b, v_blk0 + h)),
                  pl.BlockSpec(memory_space=pl.ANY)],
        out_specs=pl.BlockSpec((lc, g * hd), lambda b, h: (b, o_blk0 + h)),
        out_shape=jax.ShapeDtypeStruct(prev.shape, prev.dtype),
        input_output_aliases={4: 0},
        compiler_params=_params(("arbitrary", "arbitrary")), name="ctx_attention",
    )(sink, qkv_c, qkv_c, qkv_c, prev)


def _log_sigmoid(x):
    return jnp.minimum(x, 0.0) - jnp.log(1.0 + jnp.exp(-jnp.abs(x)))


def _mlstm_kernel(*refs, chunk, dv, nchunks, reverse, i_off, f_off, fuse_out, qscale):
    if fuse_out:
        (q_ref, k_ref, v_ref, g_ref, c0_ref, m0_ref, hb_ref, og_ref, nw_ref,
         h_ref, ct_ref, mt_ref, c_scr, m_scr) = refs
    else:
        q_ref, k_ref, v_ref, g_ref, c0_ref, m0_ref, h_ref, ct_ref, mt_ref, c_scr, m_scr = refs
    head = pl.program_id(1)
    step = pl.program_id(2)

    @pl.when(step == 0)
    def _():
        c_scr[...] = c0_ref[...]
        m_scr[...] = m0_ref[...]

    gates = g_ref[...]
    lane = lax.broadcasted_iota(jnp.int32, gates.shape, 1)
    i_col = jnp.sum(jnp.where(lane == i_off + head, gates, 0.0), axis=1, keepdims=True)
    f_col = jnp.sum(jnp.where(lane == f_off + head, gates, 0.0), axis=1, keepdims=True)
    gates_t = gates.T
    sub = lax.broadcasted_iota(jnp.int32, gates_t.shape, 0)
    i_row = jnp.sum(jnp.where(sub == i_off + head, gates_t, 0.0), axis=0, keepdims=True)
    f_row = jnp.sum(jnp.where(sub == f_off + head, gates_t, 0.0), axis=0, keepdims=True)
    lf_col = _log_sigmoid(f_col)
    lf_row = _log_sigmoid(f_row)

    tt = lax.broadcasted_iota(jnp.int32, (chunk, chunk), 0)
    ss = lax.broadcasted_iota(jnp.int32, (chunk, chunk), 1)
    causal = (ss >= tt) if reverse else (ss <= tt)
    anti = (tt >= ss) if reverse else (tt <= ss)
    b_col = jnp.sum(jnp.where(causal, lf_row, 0.0), axis=1, keepdims=True)
    b_row = jnp.sum(jnp.where(anti, lf_col, 0.0), axis=0, keepdims=True)
    g_tot = jnp.sum(lf_row, axis=1, keepdims=True)

    m_old = m_scr[:, 0:1]
    a_row = g_tot - b_row + i_row
    a_col = g_tot - b_col + i_col
    m_new = jnp.maximum(g_tot + m_old, jnp.max(a_row, axis=1, keepdims=True))
    wk_col = jnp.exp(a_col - m_new)
    decay = jnp.exp(g_tot + m_old - m_new)

    q = q_ref[...]
    k = k_ref[...]
    v_aug = jnp.concatenate([v_ref[...], jnp.ones((chunk, LANES), BF16)], axis=1)
    c_old = c_scr[...]

    inter = jnp.dot(q, c_old.astype(BF16), preferred_element_type=F32) * qscale
    qk = lax.dot_general(q, k, _NT, preferred_element_type=F32) * qscale
    dmat = jnp.where(causal, b_col - b_row + i_row, -jnp.inf)
    m_inter = b_col + m_old
    m_t = jnp.maximum(m_inter, jnp.max(dmat, axis=1, keepdims=True))
    w_inter = jnp.exp(m_inter - m_t)
    pw = jnp.exp(dmat - m_t) * qk
    num = w_inter * inter + jnp.dot(pw.astype(BF16), v_aug, preferred_element_type=F32)
    den = num[:, dv:dv + 1]
    hc = num[:, :dv] / jnp.maximum(jnp.abs(den), jnp.exp(-m_t))

    kw = (k.astype(F32) * wk_col).astype(BF16)
    c_scr[...] = decay * c_old + lax.dot_general(kw, v_aug, _TN, preferred_element_type=F32)
    m_scr[...] = jnp.broadcast_to(m_new, m_scr.shape)

    if fuse_out:
        hs = hc + hb_ref[...]
        hn = hs * lax.rsqrt(jnp.mean(hs * hs, axis=-1, keepdims=True) + NORM_EPS) * nw_ref[...]
        h_ref[...] = (hn * jax.nn.sigmoid(og_ref[...].astype(F32))).astype(h_ref.dtype)
    else:
        h_ref[...] = hc

    @pl.when(step == nchunks - 1)
    def _():
        ct_ref[...] = c_scr[...]
        mt_ref[...] = m_scr[...]


def _mlstm_scan(proj, gates, state, n_tok, reverse, fuse=None):
    nh = MLSTM_HEADS
    m = proj.shape[0]
    dv = proj.shape[1] // (3 * nh)
    dqk = dv // 2
    batch = m // n_tok
    chunk = min(MLSTM_KERNEL_CHUNK, n_tok)
    nchunks = n_tok // chunk
    c0, m0 = state
    cidx = (lambda c: nchunks - 1 - c) if reverse else (lambda c: c)
    row = lambda b, c: b * nchunks + cidx(c)
    i_off, f_off = (2 * nh, 3 * nh) if reverse else (0, nh)
    in_specs = [pl.BlockSpec((chunk, dqk), lambda b, h, c: (row(b, c), h)),
                pl.BlockSpec((chunk, dqk), lambda b, h, c: (row(b, c), nh + h)),
                pl.BlockSpec((chunk, dv), lambda b, h, c: (row(b, c), nh + h)),
                pl.BlockSpec((chunk, LANES), lambda b, h, c: (row(b, c), 0)),
                pl.BlockSpec((None, None, dqk, dv + LANES), lambda b, h, c: (b, h, 0, 0)),
                pl.BlockSpec((None, None, 1, LANES), lambda b, h, c: (b, h, 0, 0))]
    args = [proj, proj, proj, gates, c0, m0]
    if fuse is not None:
        h_other, norm_w = fuse
        in_specs += [pl.BlockSpec((chunk, dv), lambda b, h, c: (row(b, c), h)),
                     pl.BlockSpec((chunk, dv), lambda b, h, c: (row(b, c), 2 * nh + h)),
                     pl.BlockSpec((1, dv), lambda b, h, c: (0, h))]
        args += [h_other, proj, norm_w.reshape(1, nh * dv)]
    kern = functools.partial(_mlstm_kernel, chunk=chunk, dv=dv, nchunks=nchunks, reverse=reverse,
                             i_off=i_off, f_off=f_off, fuse_out=fuse is not None, qscale=dqk ** -0.5)
    return pl.pallas_call(
        kern, grid=(batch, nh, nchunks), in_specs=in_specs,
        out_specs=[pl.BlockSpec((chunk, dv), lambda b, h, c: (row(b, c), h)),
                   pl.BlockSpec((None, None, dqk, dv + LANES), lambda b, h, c: (b, h, 0, 0)),
                   pl.BlockSpec((None, None, 1, LANES), lambda b, h, c: (b, h, 0, 0))],
        out_shape=[jax.ShapeDtypeStruct((m, nh * dv), F32 if fuse is None else BF16),
                   jax.ShapeDtypeStruct(c0.shape, F32),
                   jax.ShapeDtypeStruct(m0.shape, F32)],
        scratch_shapes=[pltpu.VMEM((dqk, dv + LANES), F32), pltpu.VMEM((1, LANES), F32)],
        compiler_params=_params(("arbitrary", "arbitrary", "arbitrary")),
        name="mlstm_scan_bwd" if reverse else "mlstm_scan_fwd")(*args)


def _expert_ffn_kernel(x_ref, w1_ref, w3_ref, w2_ref, g_ref, o_ref):
    x = x_ref[...]
    a = jnp.dot(x, w1_ref[...], preferred_element_type=F32)
    u = jnp.dot(x, w3_ref[...], preferred_element_type=F32)
    hmid = (a * jax.nn.sigmoid(a) * u).astype(BF16)
    o_ref[...] = jnp.dot(hmid, w2_ref[...], preferred_element_type=F32) * g_ref[...]


def _expert_ffn(xs, gate, w1, w3, w2):
    batch, n_exp, cap, d = xs.shape
    ff = w1.shape[-1]
    tc = _pick(cap, (512, 256, 128, 64, 32, 16, 8))
    return pl.pallas_call(
        _expert_ffn_kernel, grid=(n_exp, batch, cap // tc),
        in_specs=[pl.BlockSpec((None, None, tc, d), lambda e, b, j: (b, e, j, 0)),
                  pl.BlockSpec((None, d, ff), lambda e, b, j: (e, 0, 0)),
                  pl.BlockSpec((None, d, ff), lambda e, b, j: (e, 0, 0)),
                  pl.BlockSpec((None, ff, d), lambda e, b, j: (e, 0, 0)),
                  pl.BlockSpec((None, None, tc, 1), lambda e, b, j: (b, e, j, 0))],
        out_specs=pl.BlockSpec((None, None, tc, d), lambda e, b, j: (b, e, j, 0)),
        out_shape=jax.ShapeDtypeStruct((batch, n_exp, cap, d), F32),
        compiler_params=_params(("arbitrary", "arbitrary", "arbitrary")), name="expert_ffn",
    )(xs, w1, w3, w2, gate[..., None])


def _moe(x, norm_w, shift, scale, gate2, router_t, w1, w3, w2, n_tok):
    m, d = x.shape
    batch = m // n_tok
    n_exp = router_t.shape[0]
    cap = (CAPACITY_FACTOR * n_tok) // n_exp
    h2, aff = _norm_mod(x, norm_w, shift, scale, n_tok, router_t=router_t)
    gate, idx = lax.top_k(aff, cap)
    rows = idx + (jnp.arange(batch, dtype=idx.dtype) * n_tok)[:, None, None]
    xs = jnp.take(h2, rows.reshape(-1), axis=0).reshape(batch, n_exp, cap, d)
    y = _expert_ffn(xs, gate, w1, w3, w2)
    moe = jnp.zeros((m, d), F32).at[rows.reshape(-1)].add(y.reshape(-1, d))
    g2 = jnp.repeat(gate2[:, 0, :], n_tok, axis=0) if gate2.shape[0] > 1 else gate2[0]
    return x + g2 * moe


def kernel(x, c, ctx, c_ctx, ada_w, ada_b, norm1_w, norm2_w, ab_w_in, ab_w_out, na_rpb, win_sink,
           ml_w_in, ml_b_gates, ml_norm_w, ml_w_out, moe_router, moe_w1, moe_w3, moe_w2, final_norm_w):
    batch, n, d = x.shape
    lc = ctx.shape[1]
    depth = ada_w.shape[0]
    hd = HEAD_DIM
    n_heads = d // hd
    n_na = n_heads // 2
    n_q = n_heads - n_na
    n_kv = max(1, n_q // 4)
    nh = MLSTM_HEADS
    dv = d // nh
    dqk = dv // 2
    assert batch + 1 <= 8

    xl = x.reshape(batch * n, d)
    xc = ctx.reshape(batch * lc, d)
    cc = jnp.concatenate([c, c_ctx[None], jnp.zeros((8 - batch - 1, d), F32)], axis=0)
    mods = _ada_mods(cc, ada_w, ada_b)
    cos, sin = _rope_tables(n)

    for l in range(depth):
        need_ctx = l < depth - 1
        ml = mods[l].reshape(8, ADA_MODS, 1, d)
        sh1, sc1, g1, sh2, sc2, g2 = (ml[:batch, j] for j in range(ADA_MODS))
        ch1, cs1, cg1, ch2, cs2, cg2 = (ml[batch:batch + 1, j] for j in range(ADA_MODS))
        hl = _norm_mod(xl, norm1_w[l], sh1, sc1, n)
        hc = _norm_mod(xc, norm1_w[l], ch1, cs1, lc)
        if l % 2 == 0:
            e = l // 2
            w_in = ab_w_in[e].astype(BF16)
            w_out = ab_w_out[e].astype(BF16)
            qkv_l = _matmul(hl, w_in)
            qkv_c = _matmul(hc, w_in)
            bias = _na_bias_table(na_rpb[e], n // GRID_W)
            attn_l = _na_attention(qkv_l, qkv_c, bias, batch, n, lc, n_na, (n_na + n_q) * hd)
            attn_l = _win_attention(attn_l, win_sink[e], qkv_l, qkv_c, cos, sin, batch, n, lc, n_na, n_q, n_kv)
            xl = _matmul_residual(attn_l, w_out, xl, g1, n)
            if need_ctx:
                g = n_q // n_kv
                attn_c = jnp.zeros((batch * lc, (n_na + n_q) * hd), BF16)
                attn_c = _ctx_attention(attn_c, win_sink[e], qkv_c, batch, lc, n_na, 1,
                                        0, n_na, 2 * n_na, 0, False)
                attn_c = _ctx_attention(attn_c, win_sink[e], qkv_c, batch, lc, n_kv, g,
                                        (3 * n_na) // g, 3 * n_na + n_q, 3 * n_na + n_q + n_kv, n_na // g, True)
                xc = _matmul_residual(attn_c, w_out, xc, cg1, lc)
        else:
            o = l // 2
            main_w = 2 * nh * dqk + 2 * nh * dv
            w_main = ml_w_in[o][:, :main_w].astype(BF16)
            n_gates = ml_w_in.shape[2] - main_w
            w_gate = jnp.pad(ml_w_in[o][:, main_w:], ((0, 0), (0, LANES - n_gates))).astype(BF16)
            b_gate = jnp.pad(ml_b_gates[o], (0, LANES - n_gates))
            w_out = ml_w_out[o].astype(BF16)
            pl_l, pl_c = _matmul(hl, w_main), _matmul(hc, w_main)
            gt_l = _matmul(hl, w_gate, out_dtype=F32, bias=b_gate)
            gt_c = _matmul(hc, w_gate, out_dtype=F32, bias=b_gate)
            state0 = (jnp.zeros((batch, nh, dqk, dv + LANES), F32), jnp.zeros((batch, nh, 1, LANES), F32))
            hcb, cb, mb = _mlstm_scan(pl_c, gt_c, state0, lc, True)
            hlb, _, _ = _mlstm_scan(pl_l, gt_l, (cb, mb), n, True)
            yc, cf, mf = _mlstm_scan(pl_c, gt_c, state0, lc, False, fuse=(hcb, ml_norm_w[o]))
            yl, _, _ = _mlstm_scan(pl_l, gt_l, (cf, mf), n, False, fuse=(hlb, ml_norm_w[o]))
            xl = _matmul_residual(yl, w_out, xl, g1, n)
            if need_ctx:
                xc = _matmul_residual(yc, w_out, xc, cg1, lc)
        router_t = moe_router[l].T.astype(BF16)
        w1, w3, w2 = moe_w1[l].astype(BF16), moe_w3[l].astype(BF16), moe_w2[l].astype(BF16)
        xl = _moe(xl, norm2_w[l], sh2, sc2, g2, router_t, w1, w3, w2, n)
        if need_ctx:
            xc = _moe(xc, norm2_w[l], ch2, cs2, cg2, router_t, w1, w3, w2, lc)

    zeros = jnp.zeros((1, 1, d), F32)
    out = _norm_mod(xl, final_norm_w, zeros, zeros, n, out_dtype=F32)
    return out.reshape(batch, n, d)
```

```python
import functools

import numpy as np
import jax
import jax.numpy as jnp
from jax import lax
from jax.experimental import pallas as pl
from jax.experimental.pallas import tpu as pltpu

GRID_W = 64
HEAD_DIM = 128
NA_ROWS = 8
NA_COLS = 16
WINDOW = 128
WIN_BLOCK = 128
ROPE_BASE = 10000.0
MLSTM_HEADS = 8
MLSTM_KERNEL_CHUNK = 256
CAPACITY_FACTOR = 2
ADA_MODS = 6
NORM_EPS = 1e-6
NEG_INF = -1e30
LANES = 128
DMA_UNROLL = 8
V7X_VMEM_LIMIT_BYTES = 56 * 1024 * 1024

F32 = jnp.float32
BF16 = jnp.bfloat16
_NT = (((1,), (1,)), ((), ()))
_TN = (((0,), (0,)), ((), ()))


def _pick(n, cands):
    for c in cands:
        if n % c == 0:
            return c
    raise ValueError(f"no tile in {cands} divides {n}")


def _params(sem):
    return pltpu.CompilerParams(dimension_semantics=sem, vmem_limit_bytes=V7X_VMEM_LIMIT_BYTES)


def _ada_kernel(cc_ref, w_ref, b_ref, o_ref):
    x = cc_ref[...]
    s = x * jax.nn.sigmoid(x)
    o_ref[...] = jnp.dot(s.astype(BF16), w_ref[...].astype(BF16), preferred_element_type=F32) + b_ref[...]


def _ada_mods(cc, ada_w, ada_b):
    depth, d, n = ada_w.shape
    tn = _pick(n, (512, 256, 128))
    return pl.pallas_call(
        _ada_kernel,
        grid=(depth, n // tn),
        in_specs=[pl.BlockSpec((8, d), lambda l, j: (0, 0)),
                  pl.BlockSpec((None, d, tn), lambda l, j: (l, 0, j)),
                  pl.BlockSpec((None, 1, tn), lambda l, j: (l, 0, j))],
        out_specs=pl.BlockSpec((None, 8, tn), lambda l, j: (l, 0, j)),
        out_shape=jax.ShapeDtypeStruct((depth, 8, n), F32),
        compiler_params=_params(("arbitrary", "arbitrary")),
        name="ada_mods",
    )(cc, ada_w, ada_b.reshape(depth, 1, n))


def _norm_body(x_ref, w_ref, sh_ref, sc_ref):
    x = x_ref[...]
    y = x * lax.rsqrt(jnp.mean(x * x, axis=-1, keepdims=True) + NORM_EPS) * w_ref[...]
    return y * (1.0 + sc_ref[...]) + sh_ref[...]


def _norm_kernel(x_ref, w_ref, sh_ref, sc_ref, o_ref):
    o_ref[...] = _norm_body(x_ref, w_ref, sh_ref, sc_ref).astype(o_ref.dtype)


def _norm_router_kernel(x_ref, w_ref, sh_ref, sc_ref, wr_ref, o_ref, aff_ref):
    h = _norm_body(x_ref, w_ref, sh_ref, sc_ref)
    o_ref[...] = h
    logits = lax.dot_general(wr_ref[...], h.astype(BF16), _NT, preferred_element_type=F32)
    e = jnp.exp(logits - jnp.max(logits, axis=0, keepdims=True))
    aff_ref[...] = e / jnp.sum(e, axis=0, keepdims=True)


def _norm_mod(x, w, shift, scale, n_tok, out_dtype=BF16, router_t=None):
    m, d = x.shape
    tm = _pick(n_tok, (256, 128))
    tiles = n_tok // tm
    per_sample = shift.shape[0] > 1
    mod_idx = (lambda i: (i // tiles, 0, 0)) if per_sample else (lambda i: (0, 0, 0))
    in_specs = [pl.BlockSpec((tm, d), lambda i: (i, 0)),
                pl.BlockSpec((1, d), lambda i: (0, 0)),
                pl.BlockSpec((None, 1, d), mod_idx),
                pl.BlockSpec((None, 1, d), mod_idx)]
    args = [x, w.reshape(1, d), shift, scale]
    o_spec = pl.BlockSpec((tm, d), lambda i: (i, 0))
    if router_t is None:
        return pl.pallas_call(
            _norm_kernel, grid=(m // tm,), in_specs=in_specs, out_specs=o_spec,
            out_shape=jax.ShapeDtypeStruct((m, d), out_dtype),
            compiler_params=_params(("arbitrary",)), name="norm_mod")(*args)
    n_exp = router_t.shape[0]
    return pl.pallas_call(
        _norm_router_kernel, grid=(m // tm,),
        in_specs=in_specs + [pl.BlockSpec((n_exp, d), lambda i: (0, 0))],
        out_specs=[o_spec, pl.BlockSpec((None, n_exp, tm), lambda i: (i // tiles, 0, i % tiles))],
        out_shape=[jax.ShapeDtypeStruct((m, d), F32),
                   jax.ShapeDtypeStruct((m // n_tok, n_exp, n_tok), F32)],
        compiler_params=_params(("arbitrary",)), name="norm_mod_router")(*args, router_t)


def _cast_weight_once(w_ref, wb_scr):
    @pl.when(pl.program_id(1) == 0)
    def _():
        wb_scr[...] = w_ref[...].astype(BF16)


def _proj_kernel(a_ref, w_ref, o_ref, wb_scr):
    _cast_weight_once(w_ref, wb_scr)
    o_ref[...] = jnp.dot(a_ref[...], wb_scr[...], preferred_element_type=F32).astype(o_ref.dtype)


def _proj_res_kernel(*refs):
    *a_refs, w_ref, r_ref, g_ref, o_ref, wb_scr = refs
    _cast_weight_once(w_ref, wb_scr)
    acc, k0 = None, 0
    for a_ref in a_refs:
        k1 = k0 + a_ref.shape[1]
        part = jnp.dot(a_ref[...], wb_scr[k0:k1, :], preferred_element_type=F32)
        acc, k0 = part if acc is None else acc + part, k1
    o_ref[...] = r_ref[...] + g_ref[...] * acc


def _mm_bias_kernel(a_ref, w_ref, b_ref, o_ref):
    o_ref[...] = jnp.dot(a_ref[...], w_ref[...], preferred_element_type=F32) + b_ref[...]


def _proj(a, w_stack, layer, n_cols, out_dtype=BF16):
    m, k = a.shape
    tm = _pick(m, (1024, 512, 256, 128))
    tn = _pick(n_cols, (512, 256, 128))
    return pl.pallas_call(
        _proj_kernel, grid=(n_cols // tn, m // tm),
        in_specs=[pl.BlockSpec((tm, k), lambda j, i: (i, 0)),
                  pl.BlockSpec((None, k, tn), lambda j, i: (layer, 0, j))],
        out_specs=pl.BlockSpec((tm, tn), lambda j, i: (i, j)),
        out_shape=jax.ShapeDtypeStruct((m, n_cols), out_dtype),
        scratch_shapes=[pltpu.VMEM((k, tn), BF16)],
        compiler_params=_params(("arbitrary", "arbitrary")), name="proj")(a, w_stack)


def _proj_residual(a_parts, w_stack, layer, res, gate, n_tok):
    m = a_parts[0].shape[0]
    _, k, n = w_stack.shape
    assert sum(a.shape[1] for a in a_parts) == k
    tm = _pick(n_tok, (1024, 512, 256, 128))
    tn = _pick(n, (512, 256, 128))
    tiles = n_tok // tm
    g_idx = (lambda j, i: (i // tiles, 0, j)) if gate.shape[0] > 1 else (lambda j, i: (0, 0, j))
    return pl.pallas_call(
        _proj_res_kernel, grid=(n // tn, m // tm),
        in_specs=[pl.BlockSpec((tm, a.shape[1]), lambda j, i: (i, 0)) for a in a_parts]
                 + [pl.BlockSpec((None, k, tn), lambda j, i: (layer, 0, j)),
                    pl.BlockSpec((tm, tn), lambda j, i: (i, j)),
                    pl.BlockSpec((None, 1, tn), g_idx)],
        out_specs=pl.BlockSpec((tm, tn), lambda j, i: (i, j)),
        out_shape=jax.ShapeDtypeStruct((m, n), F32),
        scratch_shapes=[pltpu.VMEM((k, tn), BF16)],
        input_output_aliases={len(a_parts) + 1: 0},
        compiler_params=_params(("arbitrary", "arbitrary")), name="proj_residual")(*a_parts, w_stack, res, gate)


def _gate_proj(a, w, bias):
    m, k = a.shape
    n = w.shape[1]
    tm = _pick(m, (1024, 512, 256, 128))
    return pl.pallas_call(
        _mm_bias_kernel, grid=(m // tm,),
        in_specs=[pl.BlockSpec((tm, k), lambda i: (i, 0)), pl.BlockSpec((k, n), lambda i: (0, 0)),
                  pl.BlockSpec((1, n), lambda i: (0, 0))],
        out_specs=pl.BlockSpec((tm, n), lambda i: (i, 0)),
        out_shape=jax.ShapeDtypeStruct((m, n), F32),
        compiler_params=_params(("arbitrary",)), name="gate_proj")(a, w, bias.reshape(1, n))


def _na_bias_table(rpb, rows):
    kh = min(NA_ROWS, rows)
    cols = np.arange(GRID_W)
    cstart = np.clip(cols - NA_COLS // 2, 0, GRID_W - NA_COLS)
    valid = (cols[None, :] >= cstart[:, None]) & (cols[None, :] < cstart[:, None] + NA_COLS)
    dc = np.clip(cols[None, :] - cols[:, None] + NA_COLS - 1, 0, 2 * NA_COLS - 2)
    dr = np.arange(kh)[None, :] - np.arange(kh)[:, None] + NA_ROWS - 1
    b = rpb[:, dr[:, None, :, None], dc[None, :, None, :]]
    b = jnp.where(valid[None, None, :, None, :], b.astype(F32), NEG_INF)
    return b.reshape(rpb.shape[0], kh, GRID_W, kh * GRID_W)


def _na_kernel(q_ref, k_ref, v_ref, kc_ref, vc_ref, bias_ref, o_ref, *, rb, rows, kh, scale):
    r0 = pl.program_id(2) * rb
    kc = kc_ref[...]
    vc = vc_ref[...]
    for i in range(rb):
        r = r0 + i
        rs = jnp.clip(r - kh // 2, 0, rows - kh)
        start = pl.multiple_of(rs * GRID_W, GRID_W)
        q = q_ref[i * GRID_W:(i + 1) * GRID_W, :]
        kw = k_ref[pl.ds(start, kh * GRID_W), :]
        vw = v_ref[pl.ds(start, kh * GRID_W), :]
        s = lax.dot_general(q, kw, _NT, preferred_element_type=F32) * scale + bias_ref[r - rs]
        sc = lax.dot_general(q, kc, _NT, preferred_element_type=F32) * scale
        mx = jnp.maximum(jnp.max(s, axis=-1, keepdims=True), jnp.max(sc, axis=-1, keepdims=True))
        p = jnp.exp(s - mx)
        pc = jnp.exp(sc - mx)
        den = jnp.sum(p, axis=-1, keepdims=True) + jnp.sum(pc, axis=-1, keepdims=True)
        o = (jnp.dot(p.astype(BF16), vw, preferred_element_type=F32)
             + jnp.dot(pc.astype(BF16), vc, preferred_element_type=F32))
        o_ref[i * GRID_W:(i + 1) * GRID_W, :] = (o / den).astype(o_ref.dtype)


def _na_attention(qkv_l, qkv_c, bias, batch, n, lc, n_heads):
    rows = n // GRID_W
    kh = min(NA_ROWS, rows)
    rb = _pick(rows, (8, 4, 2, 1))
    nrb = rows // rb
    hd = HEAD_DIM
    kern = functools.partial(_na_kernel, rb=rb, rows=rows, kh=kh, scale=hd ** -0.5)
    return pl.pallas_call(
        kern, grid=(batch, n_heads, nrb),
        in_specs=[pl.BlockSpec((rb * GRID_W, hd), lambda b, h, r: (b * nrb + r, h)),
                  pl.BlockSpec((n, hd), lambda b, h, r: (b, n_heads + h)),
                  pl.BlockSpec((n, hd), lambda b, h, r: (b, 2 * n_heads + h)),
                  pl.BlockSpec((lc, hd), lambda b, h, r: (b, n_heads + h)),
                  pl.BlockSpec((lc, hd), lambda b, h, r: (b, 2 * n_heads + h)),
                  pl.BlockSpec((None, kh, GRID_W, kh * GRID_W), lambda b, h, r: (h, 0, 0, 0))],
        out_specs=pl.BlockSpec((rb * GRID_W, hd), lambda b, h, r: (b * nrb + r, h)),
        out_shape=jax.ShapeDtypeStruct((batch * n, n_heads * hd), BF16),
        compiler_params=_params(("arbitrary", "arbitrary", "arbitrary")), name="na_attention",
    )(qkv_l, qkv_l, qkv_l, qkv_c, qkv_c, bias)


def _rope_tables(n):
    nf = HEAD_DIM // 4
    t = jnp.arange(n)
    inv = ROPE_BASE ** (-jnp.arange(nf, dtype=F32) / nf)
    ang_r = (t // GRID_W).astype(F32)[:, None] * inv
    ang_c = (t % GRID_W).astype(F32)[:, None] * inv
    cos = jnp.concatenate([jnp.cos(ang_r), jnp.cos(ang_r), jnp.cos(ang_c), jnp.cos(ang_c)], axis=-1)
    sin = jnp.concatenate([-jnp.sin(ang_r), jnp.sin(ang_r), -jnp.sin(ang_c), jnp.sin(ang_c)], axis=-1)
    return cos, sin


def _rope(x, cos, sin_signed):
    nf = HEAD_DIM // 4
    lane = lax.broadcasted_iota(jnp.int32, x.shape, 1)
    first = (lane % (2 * nf)) < nf
    swapped = jnp.where(first, pltpu.roll(x, HEAD_DIM - nf, 1), pltpu.roll(x, nf, 1))
    return x * cos + swapped * sin_signed


def _sink_column(sink_ref, base, g, rows_per_head):
    return jnp.concatenate([jnp.full((rows_per_head, 1), sink_ref[base + j], F32) for j in range(g)], axis=0)


def _win_kernel(sink_ref, q_ref, k_ref, v_ref, kc_ref, vc_ref, cq_ref, sq_ref, ck_ref, sk_ref, o_ref,
                *, g, n, scale):
    hd = HEAD_DIM
    kvh = pl.program_id(1)
    blk = pl.program_id(2)
    span = WIN_BLOCK + 2 * WINDOW
    s0 = pl.multiple_of(jnp.clip(blk * WIN_BLOCK - WINDOW, 0, n - span), LANES)
    cq = cq_ref[...]
    sq = sq_ref[...]
    q = q_ref[...]
    qs = jnp.concatenate([_rope(q[:, j * hd:(j + 1) * hd].astype(F32), cq, sq) for j in range(g)],
                         axis=0).astype(BF16)
    kw = _rope(k_ref[pl.ds(s0, span), :].astype(F32), ck_ref[pl.ds(s0, span), :],
               sk_ref[pl.ds(s0, span), :]).astype(BF16)
    vw = v_ref[pl.ds(s0, span), :]
    s = lax.dot_general(qs, kw, _NT, preferred_element_type=F32) * scale
    qpos = blk * WIN_BLOCK + lax.broadcasted_iota(jnp.int32, (g * WIN_BLOCK, 1), 0) % WIN_BLOCK
    kpos = s0 + lax.broadcasted_iota(jnp.int32, (1, span), 1)
    s = jnp.where(jnp.abs(qpos - kpos) <= WINDOW, s, NEG_INF)
    sc = lax.dot_general(qs, kc_ref[...], _NT, preferred_element_type=F32) * scale
    sink = _sink_column(sink_ref, kvh * g, g, WIN_BLOCK)
    mx = jnp.maximum(jnp.maximum(jnp.max(s, axis=-1, keepdims=True), jnp.max(sc, axis=-1, keepdims=True)), sink)
    p = jnp.exp(s - mx)
    pc = jnp.exp(sc - mx)
    den = jnp.sum(p, axis=-1, keepdims=True) + jnp.sum(pc, axis=-1, keepdims=True) + jnp.exp(sink - mx)
    o = (jnp.dot(p.astype(BF16), vw, preferred_element_type=F32)
         + jnp.dot(pc.astype(BF16), vc_ref[...], preferred_element_type=F32)) / den
    o_ref[...] = jnp.concatenate([o[j * WIN_BLOCK:(j + 1) * WIN_BLOCK] for j in range(g)],
                                 axis=1).astype(o_ref.dtype)


def _win_attention(sink, qkv_l, qkv_c, cos, sin, batch, n, lc, n_na, n_q, n_kv):
    hd = HEAD_DIM
    g = n_q // n_kv
    nb = n // WIN_BLOCK
    q_blk0 = (3 * n_na * hd) // (g * hd)
    k_blk0 = 3 * n_na + n_q
    assert (3 * n_na) % g == 0
    kern = functools.partial(_win_kernel, g=g, n=n, scale=hd ** -0.5)
    full = lambda b, kv, i: (0, 0)
    return pl.pallas_call(
        kern, grid=(batch, n_kv, nb),
        in_specs=[pl.BlockSpec(memory_space=pltpu.SMEM),
                  pl.BlockSpec((WIN_BLOCK, g * hd), lambda b, kv, i: (b * nb + i, q_blk0 + kv)),
                  pl.BlockSpec((n, hd), lambda b, kv, i: (b, k_blk0 + kv)),
                  pl.BlockSpec((n, hd), lambda b, kv, i: (b, k_blk0 + n_kv + kv)),
                  pl.BlockSpec((lc, hd), lambda b, kv, i: (b, k_blk0 + kv)),
                  pl.BlockSpec((lc, hd), lambda b, kv, i: (b, k_blk0 + n_kv + kv)),
                  pl.BlockSpec((WIN_BLOCK, hd), lambda b, kv, i: (i, 0)),
                  pl.BlockSpec((WIN_BLOCK, hd), lambda b, kv, i: (i, 0)),
                  pl.BlockSpec((n, hd), full),
                  pl.BlockSpec((n, hd), full)],
        out_specs=pl.BlockSpec((WIN_BLOCK, g * hd), lambda b, kv, i: (b * nb + i, kv)),
        out_shape=jax.ShapeDtypeStruct((batch * n, n_q * hd), BF16),
        compiler_params=_params(("arbitrary", "arbitrary", "arbitrary")), name="window_attention",
    )(sink, qkv_l, qkv_l, qkv_l, qkv_c, qkv_c, cos, sin, cos, sin)


def _ctx_attn_kernel(sink_ref, q_ref, k_ref, v_ref, o_ref, *, g, lc, use_sink, scale):
    hd = HEAD_DIM
    q = q_ref[...]
    qs = jnp.concatenate([q[:, j * hd:(j + 1) * hd] for j in range(g)], axis=0)
    s = lax.dot_general(qs, k_ref[...], _NT, preferred_element_type=F32) * scale
    mx = jnp.max(s, axis=-1, keepdims=True)
    if use_sink:
        sink = _sink_column(sink_ref, pl.program_id(1) * g, g, lc)
        mx = jnp.maximum(mx, sink)
    p = jnp.exp(s - mx)
    den = jnp.sum(p, axis=-1, keepdims=True)
    if use_sink:
        den = den + jnp.exp(sink - mx)
    o = jnp.dot(p.astype(BF16), v_ref[...], preferred_element_type=F32) / den
    o_ref[...] = jnp.concatenate([o[j * lc:(j + 1) * lc] for j in range(g)], axis=1).astype(o_ref.dtype)


def _ctx_attention(sink, qkv_c, batch, lc, n_groups, g, q_blk0, k_blk0, v_blk0, use_sink):
    hd = HEAD_DIM
    kern = functools.partial(_ctx_attn_kernel, g=g, lc=lc, use_sink=use_sink, scale=hd ** -0.5)
    return pl.pallas_call(
        kern, grid=(batch, n_groups),
        in_specs=[pl.BlockSpec(memory_space=pltpu.SMEM),
                  pl.BlockSpec((lc, g * hd), lambda b, h: (b, q_blk0 + h)),
                  pl.BlockSpec((lc, hd), lambda b, h: (b, k_blk0 + h)),
                  pl.BlockSpec((lc, hd), lambda b, h: (b, v_blk0 + h))],
        out_specs=pl.BlockSpec((lc, g * hd), lambda b, h: (b, h)),
        out_shape=jax.ShapeDtypeStruct((batch * lc, n_groups * g * hd), BF16),
        compiler_params=_params(("arbitrary", "arbitrary")), name="ctx_attention",
    )(sink, qkv_c, qkv_c, qkv_c)


def _log_sigmoid(x):
    return jnp.minimum(x, 0.0) - jnp.log(1.0 + jnp.exp(-jnp.abs(x)))


def _mlstm_kernel(*refs, chunk, dv, nchunks, reverse, i_off, f_off, fuse_out, qscale):
    if fuse_out:
        (q_ref, k_ref, v_ref, g_ref, c0_ref, m0_ref, hb_ref, og_ref, nw_ref,
         h_ref, ct_ref, mt_ref, c_scr, m_scr) = refs
    else:
        q_ref, k_ref, v_ref, g_ref, c0_ref, m0_ref, h_ref, ct_ref, mt_ref, c_scr, m_scr = refs
    head = pl.program_id(1)
    step = pl.program_id(2)

    @pl.when(step == 0)
    def _():
        c_scr[...] = c0_ref[...]
        m_scr[...] = m0_ref[...]

    gates = g_ref[...]
    lane = lax.broadcasted_iota(jnp.int32, gates.shape, 1)
    i_col = jnp.sum(jnp.where(lane == i_off + head, gates, 0.0), axis=1, keepdims=True)
    f_col = jnp.sum(jnp.where(lane == f_off + head, gates, 0.0), axis=1, keepdims=True)
    gates_t = gates.T
    sub = lax.broadcasted_iota(jnp.int32, gates_t.shape, 0)
    i_row = jnp.sum(jnp.where(sub == i_off + head, gates_t, 0.0), axis=0, keepdims=True)
    f_row = jnp.sum(jnp.where(sub == f_off + head, gates_t, 0.0), axis=0, keepdims=True)
    lf_col = _log_sigmoid(f_col)
    lf_row = _log_sigmoid(f_row)

    tt = lax.broadcasted_iota(jnp.int32, (chunk, chunk), 0)
    ss = lax.broadcasted_iota(jnp.int32, (chunk, chunk), 1)
    causal = (ss >= tt) if reverse else (ss <= tt)
    anti = (tt >= ss) if reverse else (tt <= ss)
    b_col = jnp.sum(jnp.where(causal, lf_row, 0.0), axis=1, keepdims=True)
    b_row = jnp.sum(jnp.where(anti, lf_col, 0.0), axis=0, keepdims=True)
    g_tot = jnp.sum(lf_row, axis=1, keepdims=True)

    m_old = m_scr[:, 0:1]
    a_row = g_tot - b_row + i_row
    a_col = g_tot - b_col + i_col
    m_new = jnp.maximum(g_tot + m_old, jnp.max(a_row, axis=1, keepdims=True))
    wk_col = jnp.exp(a_col - m_new)
    decay = jnp.exp(g_tot + m_old - m_new)

    q = q_ref[...]
    k = k_ref[...]
    v_aug = jnp.concatenate([v_ref[...], jnp.ones((chunk, LANES), BF16)], axis=1)
    c_old = c_scr[...]

    inter = jnp.dot(q, c_old.astype(BF16), preferred_element_type=F32) * qscale
    qk = lax.dot_general(q, k, _NT, preferred_element_type=F32) * qscale
    dmat = jnp.where(causal, b_col - b_row + i_row, -jnp.inf)
    m_inter = b_col + m_old
    m_t = jnp.maximum(m_inter, jnp.max(dmat, axis=1, keepdims=True))
    w_inter = jnp.exp(m_inter - m_t)
    pw = jnp.exp(dmat - m_t) * qk
    num = w_inter * inter + jnp.dot(pw.astype(BF16), v_aug, preferred_element_type=F32)
    den = num[:, dv:dv + 1]
    hc = num[:, :dv] / jnp.maximum(jnp.abs(den), jnp.exp(-m_t))

    kw = (k.astype(F32) * wk_col).astype(BF16)
    c_scr[...] = decay * c_old + lax.dot_general(kw, v_aug, _TN, preferred_element_type=F32)
    m_scr[...] = jnp.broadcast_to(m_new, m_scr.shape)

    if fuse_out:
        hs = hc + hb_ref[...]
        hn = hs * lax.rsqrt(jnp.mean(hs * hs, axis=-1, keepdims=True) + NORM_EPS) * nw_ref[...]
        h_ref[...] = (hn * jax.nn.sigmoid(og_ref[...].astype(F32))).astype(h_ref.dtype)
    else:
        h_ref[...] = hc

    @pl.when(step == nchunks - 1)
    def _():
        ct_ref[...] = c_scr[...]
        mt_ref[...] = m_scr[...]


def _mlstm_scan(proj, gates, state, n_tok, reverse, fuse=None):
    nh = MLSTM_HEADS
    m = proj.shape[0]
    dv = proj.shape[1] // (3 * nh)
    dqk = dv // 2
    batch = m // n_tok
    chunk = min(MLSTM_KERNEL_CHUNK, n_tok)
    nchunks = n_tok // chunk
    c0, m0 = state
    cidx = (lambda c: nchunks - 1 - c) if reverse else (lambda c: c)
    row = lambda b, c: b * nchunks + cidx(c)
    i_off, f_off = (2 * nh, 3 * nh) if reverse else (0, nh)
    in_specs = [pl.BlockSpec((chunk, dqk), lambda b, h, c: (row(b, c), h)),
                pl.BlockSpec((chunk, dqk), lambda b, h, c: (row(b, c), nh + h)),
                pl.BlockSpec((chunk, dv), lambda b, h, c: (row(b, c), nh + h)),
                pl.BlockSpec((chunk, LANES), lambda b, h, c: (row(b, c), 0)),
                pl.BlockSpec((None, None, dqk, dv + LANES), lambda b, h, c: (b, h, 0, 0)),
                pl.BlockSpec((None, None, 1, LANES), lambda b, h, c: (b, h, 0, 0))]
    args = [proj, proj, proj, gates, c0, m0]
    if fuse is not None:
        h_other, norm_w = fuse
        in_specs += [pl.BlockSpec((chunk, dv), lambda b, h, c: (row(b, c), h)),
                     pl.BlockSpec((chunk, dv), lambda b, h, c: (row(b, c), 2 * nh + h)),
                     pl.BlockSpec((1, dv), lambda b, h, c: (0, h))]
        args += [h_other, proj, norm_w.reshape(1, nh * dv)]
    kern = functools.partial(_mlstm_kernel, chunk=chunk, dv=dv, nchunks=nchunks, reverse=reverse,
                             i_off=i_off, f_off=f_off, fuse_out=fuse is not None, qscale=dqk ** -0.5)
    return pl.pallas_call(
        kern, grid=(batch, nh, nchunks), in_specs=in_specs,
        out_specs=[pl.BlockSpec((chunk, dv), lambda b, h, c: (row(b, c), h)),
                   pl.BlockSpec((None, None, dqk, dv + LANES), lambda b, h, c: (b, h, 0, 0)),
                   pl.BlockSpec((None, None, 1, LANES), lambda b, h, c: (b, h, 0, 0))],
        out_shape=[jax.ShapeDtypeStruct((m, nh * dv), F32 if fuse is None else BF16),
                   jax.ShapeDtypeStruct(c0.shape, F32),
                   jax.ShapeDtypeStruct(m0.shape, F32)],
        scratch_shapes=[pltpu.VMEM((dqk, dv + LANES), F32), pltpu.VMEM((1, LANES), F32)],
        compiler_params=_params(("arbitrary", "arbitrary", "arbitrary")),
        name="mlstm_scan_bwd" if reverse else "mlstm_scan_fwd")(*args)


def _excl_prefix(mask_f, upper):
    n_exp, t = mask_f.shape
    off = jnp.zeros((n_exp, 1), F32)
    parts = []
    for j in range(t // LANES):
        blk = mask_f[:, j * LANES:(j + 1) * LANES]
        parts.append(jnp.dot(blk.astype(BF16), upper, preferred_element_type=F32) + off)
        off = off + jnp.sum(blk, axis=1, keepdims=True)
    return jnp.concatenate(parts, axis=1)


def _select_kernel(aff_ref, idx_ref, gate_ref, slot_scr, *, cap, n_tok):
    e = pl.program_id(1)
    n_exp, t = aff_ref.shape

    @pl.when(e == 0)
    def _():
        a = aff_ref[...]
        bits = pltpu.bitcast(a, jnp.int32)
        thr = jnp.zeros((n_exp, 1), jnp.int32)
        for bit in range(30, -1, -1):
            cand = thr | (1 << bit)
            cnt = jnp.sum(jnp.where(bits >= cand, 1.0, 0.0), axis=1, keepdims=True)
            thr = jnp.where(cnt >= cap, cand, thr)
        uu = lax.broadcasted_iota(jnp.int32, (LANES, LANES), 0)
        vv = lax.broadcasted_iota(jnp.int32, (LANES, LANES), 1)
        upper = jnp.where(uu < vv, 1.0, 0.0).astype(BF16)
        gt = jnp.where(bits > thr, 1.0, 0.0)
        eq = jnp.where(bits == thr, 1.0, 0.0)
        need = cap - jnp.sum(gt, axis=1, keepdims=True)
        sel = gt + eq * jnp.where(_excl_prefix(eq, upper) < need, 1.0, 0.0)
        slot_scr[...] = jnp.where(sel > 0.5, _excl_prefix(sel, upper), -1.0)

    a = aff_ref[pl.ds(e, 1), :]
    slot = slot_scr[pl.ds(e, 1), :]
    a_hi = a.astype(BF16).astype(F32)
    a_mid = (a - a_hi).astype(BF16).astype(F32)
    a_lo = a - a_hi - a_mid
    tpos = lax.broadcasted_iota(jnp.int32, (1, t), 1)
    t_hi = (tpos >> 6).astype(F32)
    t_lo = (tpos & 63).astype(F32)
    row = lax.broadcasted_iota(jnp.int32, (16, t), 0)
    lhs = jnp.where(row == 0, t_hi, jnp.where(row == 1, t_lo, jnp.where(row == 2, a_hi, jnp.where(
        row == 3, a_mid, jnp.where(row == 4, a_lo, 0.0))))).astype(BF16)
    s_iota = lax.broadcasted_iota(jnp.int32, (cap, t), 0).astype(F32)
    onehot = jnp.where(slot == s_iota, 1.0, 0.0).astype(BF16)
    r = lax.dot_general(lhs, onehot, _NT, preferred_element_type=F32)
    idx_ref[...] = (r[0:1] * 64.0 + r[1:2]).astype(jnp.int32) + pl.program_id(0) * n_tok
    gate_ref[...] = r[2:3] + r[3:4] + r[4:5]


def _select(aff, cap):
    batch, n_exp, n_tok = aff.shape
    kern = functools.partial(_select_kernel, cap=cap, n_tok=n_tok)
    out_spec = pl.BlockSpec((None, None, 1, cap), lambda b, e: (b, e, 0, 0))
    return pl.pallas_call(
        kern, grid=(batch, n_exp),
        in_specs=[pl.BlockSpec((None, n_exp, n_tok), lambda b, e: (b, 0, 0))],
        out_specs=[out_spec, out_spec],
        out_shape=[jax.ShapeDtypeStruct((batch, n_exp, 1, cap), jnp.int32),
                   jax.ShapeDtypeStruct((batch, n_exp, 1, cap), F32)],
        scratch_shapes=[pltpu.VMEM((n_exp, n_tok), F32)],
        compiler_params=_params(("arbitrary", "arbitrary")), name="moe_select")(aff)


def _moe_ffn_kernel(idx_ref, h_hbm, x_hbm, w1_ref, w3_ref, w2_ref, gate_ref, g2_ref, o_hbm,
                    h_buf, x_buf, sems, *, cap, n_exp):
    base = (pl.program_id(1) * n_exp + pl.program_id(0)) * cap

    def h_copy(s):
        return pltpu.make_async_copy(h_hbm.at[pl.ds(idx_ref[base + s], 1)], h_buf.at[pl.ds(s, 1)], sems.at[0])

    def x_copy(s):
        return pltpu.make_async_copy(x_hbm.at[pl.ds(idx_ref[base + s], 1)], x_buf.at[pl.ds(s, 1)], sems.at[1])

    def o_copy(s):
        return pltpu.make_async_copy(x_buf.at[pl.ds(s, 1)], o_hbm.at[pl.ds(idx_ref[base + s], 1)], sems.at[2])

    def start_in(s, carry):
        h_copy(s).start()
        x_copy(s).start()
        return carry

    def wait_in(s, carry):
        h_copy(s).wait()
        x_copy(s).wait()
        return carry

    def start_out(s, carry):
        o_copy(s).start()
        return carry

    def wait_out(s, carry):
        o_copy(s).wait()
        return carry

    lax.fori_loop(0, cap, start_in, 0, unroll=DMA_UNROLL)
    lax.fori_loop(0, cap, wait_in, 0, unroll=DMA_UNROLL)
    x = h_buf[...].astype(BF16)
    a = jnp.dot(x, w1_ref[...], preferred_element_type=F32)
    u = jnp.dot(x, w3_ref[...], preferred_element_type=F32)
    hmid = (a * jax.nn.sigmoid(a) * u).astype(BF16)
    y = jnp.dot(hmid, w2_ref[...], preferred_element_type=F32)
    ii = lax.broadcasted_iota(jnp.int32, (cap, cap), 0)
    jj = lax.broadcasted_iota(jnp.int32, (cap, cap), 1)
    gate_col = jnp.sum(jnp.where(ii == jj, gate_ref[...], 0.0), axis=1, keepdims=True)
    x_buf[...] = x_buf[...] + g2_ref[...] * (y * gate_col)
    lax.fori_loop(0, cap, start_out, 0, unroll=DMA_UNROLL)
    lax.fori_loop(0, cap, wait_out, 0, unroll=DMA_UNROLL)


def _moe_ffn(idx, gate, h2, x, w1, w3, w2, gate2):
    batch, n_exp, _, cap = idx.shape
    m, d = x.shape
    ff = w1.shape[-1]
    g_idx = (lambda e, b, i: (b, 0, 0)) if gate2.shape[0] > 1 else (lambda e, b, i: (0, 0, 0))
    kern = functools.partial(_moe_ffn_kernel, cap=cap, n_exp=n_exp)
    grid_spec = pltpu.PrefetchScalarGridSpec(
        num_scalar_prefetch=1, grid=(n_exp, batch),
        in_specs=[pl.BlockSpec(memory_space=pl.ANY),
                  pl.BlockSpec(memory_space=pl.ANY),
                  pl.BlockSpec((None, d, ff), lambda e, b, i: (e, 0, 0)),
                  pl.BlockSpec((None, d, ff), lambda e, b, i: (e, 0, 0)),
                  pl.BlockSpec((None, ff, d), lambda e, b, i: (e, 0, 0)),
                  pl.BlockSpec((None, None, 1, cap), lambda e, b, i: (b, e, 0, 0)),
                  pl.BlockSpec((None, 1, d), g_idx)],
        out_specs=pl.BlockSpec(memory_space=pl.ANY),
        scratch_shapes=[pltpu.VMEM((cap, d), F32), pltpu.VMEM((cap, d), F32), pltpu.SemaphoreType.DMA((3,))])
    return pl.pallas_call(
        kern, grid_spec=grid_spec,
        out_shape=jax.ShapeDtypeStruct((m, d), F32),
        input_output_aliases={2: 0},
        compiler_params=_params(("arbitrary", "arbitrary")), name="moe_ffn",
    )(idx.reshape(-1), h2, x, w1, w3, w2, gate, gate2)


def _moe(x, norm_w, shift, scale, gate2, router_t, w1, w3, w2, n_tok):
    n_exp = router_t.shape[0]
    cap = (CAPACITY_FACTOR * n_tok) // n_exp
    h2, aff = _norm_mod(x, norm_w, shift, scale, n_tok, router_t=router_t)
    idx, gate = _select(aff, cap)
    return _moe_ffn(idx, gate, h2, x, w1, w3, w2, gate2)


def kernel(x, c, ctx, c_ctx, ada_w, ada_b, norm1_w, norm2_w, ab_w_in, ab_w_out, na_rpb, win_sink,
           ml_w_in, ml_b_gates, ml_norm_w, ml_w_out, moe_router, moe_w1, moe_w3, moe_w2, final_norm_w):
    batch, n, d = x.shape
    lc = ctx.shape[1]
    depth = ada_w.shape[0]
    hd = HEAD_DIM
    n_heads = d // hd
    n_na = n_heads // 2
    n_q = n_heads - n_na
    n_kv = max(1, n_q // 4)
    nh = MLSTM_HEADS
    dv = d // nh
    dqk = dv // 2
    assert batch + 1 <= 8

    xl = x.reshape(batch * n, d)
    xc = ctx.reshape(batch * lc, d)
    cc = jnp.concatenate([c, c_ctx[None], jnp.zeros((8 - batch - 1, d), F32)], axis=0)
    mods = _ada_mods(cc, ada_w, ada_b)
    cos, sin = _rope_tables(n)

    for l in range(depth):
        need_ctx = l < depth - 1
        ml = mods[l].reshape(8, ADA_MODS, 1, d)
        sh1, sc1, g1, sh2, sc2, g2 = (ml[:batch, j] for j in range(ADA_MODS))
        ch1, cs1, cg1, ch2, cs2, cg2 = (ml[batch:batch + 1, j] for j in range(ADA_MODS))
        hl = _norm_mod(xl, norm1_w[l], sh1, sc1, n)
        hc = _norm_mod(xc, norm1_w[l], ch1, cs1, lc)
        if l % 2 == 0:
            e = l // 2
            qkv_l = _proj(hl, ab_w_in, e, ab_w_in.shape[2])
            qkv_c = _proj(hc, ab_w_in, e, ab_w_in.shape[2])
            bias = _na_bias_table(na_rpb[e], n // GRID_W)
            na_l = _na_attention(qkv_l, qkv_c, bias, batch, n, lc, n_na)
            win_l = _win_attention(win_sink[e], qkv_l, qkv_c, cos, sin, batch, n, lc, n_na, n_q, n_kv)
            xl = _proj_residual([na_l, win_l], ab_w_out, e, xl, g1, n)
            if need_ctx:
                g = n_q // n_kv
                na_c = _ctx_attention(win_sink[e], qkv_c, batch, lc, n_na, 1, 0, n_na, 2 * n_na, False)
                win_c = _ctx_attention(win_sink[e], qkv_c, batch, lc, n_kv, g,
                                       (3 * n_na) // g, 3 * n_na + n_q, 3 * n_na + n_q + n_kv, True)
                xc = _proj_residual([na_c, win_c], ab_w_out, e, xc, cg1, lc)
        else:
            o = l // 2
            main_w = 2 * nh * dqk + 2 * nh * dv
            n_gates = ml_w_in.shape[2] - main_w
            w_gate = jnp.pad(ml_w_in[o, :, main_w:], ((0, 0), (0, LANES - n_gates))).astype(BF16)
            b_gate = jnp.pad(ml_b_gates[o], (0, LANES - n_gates))
            pl_l, pl_c = _proj(hl, ml_w_in, o, main_w), _proj(hc, ml_w_in, o, main_w)
            gt_l = _gate_proj(hl, w_gate, b_gate)
            gt_c = _gate_proj(hc, w_gate, b_gate)
            state0 = (jnp.zeros((batch, nh, dqk, dv + LANES), F32), jnp.zeros((batch, nh, 1, LANES), F32))
            hcb, cb, mb = _mlstm_scan(pl_c, gt_c, state0, lc, True)
            hlb, _, _ = _mlstm_scan(pl_l, gt_l, (cb, mb), n, True)
            yc, cf, mf = _mlstm_scan(pl_c, gt_c, state0, lc, False, fuse=(hcb, ml_norm_w[o]))
            yl, _, _ = _mlstm_scan(pl_l, gt_l, (cf, mf), n, False, fuse=(hlb, ml_norm_w[o]))
            xl = _proj_residual([yl], ml_w_out, o, xl, g1, n)
            if need_ctx:
                xc = _proj_residual([yc], ml_w_out, o, xc, cg1, lc)
        router_t = moe_router[l].T.astype(BF16)
        w1, w3, w2 = moe_w1[l].astype(BF16), moe_w3[l].astype(BF16), moe_w2[l].astype(BF16)
        xl = _moe(xl, norm2_w[l], sh2, sc2, g2, router_t, w1, w3, w2, n)
        if need_ctx:
            xc = _moe(xc, norm2_w[l], ch2, cs2, cg2, router_t, w1, w3, w2, lc)

    zeros = jnp.zeros((1, 1, d), F32)
    out = _norm_mod(xl, final_norm_w, zeros, zeros, n, out_dtype=F32)
    return out.reshape(batch, n, d)
```

```python
import functools

import numpy as np
import jax
import jax.numpy as jnp
from jax import lax
from jax.experimental import pallas as pl
from jax.experimental.pallas import tpu as pltpu

GRID_W = 64
HEAD_DIM = 128
NA_ROWS = 8
NA_COLS = 16
NA_BLOCK_ROWS = 4
WINDOW = 128
WIN_BLOCK = 128
WIN_STEP_BLOCKS = 2
ROPE_BASE = 10000.0
MLSTM_HEADS = 8
MLSTM_KERNEL_CHUNK = 256
CAPACITY_FACTOR = 2
ADA_MODS = 6
NORM_EPS = 1e-6
NEG_INF = -1e30
LANES = 128
DMA_UNROLL = 8
V7X_VMEM_LIMIT_BYTES = 56 * 1024 * 1024

F32 = jnp.float32
BF16 = jnp.bfloat16
_NT = (((1,), (1,)), ((), ()))
_TN = (((0,), (0,)), ((), ()))


def _pick(n, cands):
    for c in cands:
        if n % c == 0:
            return c
    raise ValueError(f"no tile in {cands} divides {n}")


def _params(sem):
    return pltpu.CompilerParams(dimension_semantics=sem, vmem_limit_bytes=V7X_VMEM_LIMIT_BYTES)


def _ada_kernel(cc_ref, w_ref, b_ref, o_ref):
    x = cc_ref[...]
    s = x * jax.nn.sigmoid(x)
    o_ref[...] = jnp.dot(s.astype(BF16), w_ref[...].astype(BF16), preferred_element_type=F32) + b_ref[...]


def _ada_mods(cc, ada_w, ada_b):
    depth, d, n = ada_w.shape
    tn = _pick(n, (512, 256, 128))
    return pl.pallas_call(
        _ada_kernel,
        grid=(depth, n // tn),
        in_specs=[pl.BlockSpec((8, d), lambda l, j: (0, 0)),
                  pl.BlockSpec((None, d, tn), lambda l, j: (l, 0, j)),
                  pl.BlockSpec((None, 1, tn), lambda l, j: (l, 0, j))],
        out_specs=pl.BlockSpec((None, 8, tn), lambda l, j: (l, 0, j)),
        out_shape=jax.ShapeDtypeStruct((depth, 8, n), F32),
        compiler_params=_params(("arbitrary", "arbitrary")),
        name="ada_mods",
    )(cc, ada_w, ada_b.reshape(depth, 1, n))


def _norm_body(x_ref, w_ref, sh_ref, sc_ref):
    x = x_ref[...]
    y = x * lax.rsqrt(jnp.mean(x * x, axis=-1, keepdims=True) + NORM_EPS) * w_ref[...]
    return y * (1.0 + sc_ref[...]) + sh_ref[...]


def _norm_kernel(x_ref, w_ref, sh_ref, sc_ref, o_ref):
    o_ref[...] = _norm_body(x_ref, w_ref, sh_ref, sc_ref).astype(o_ref.dtype)


def _norm_router_kernel(x_ref, w_ref, sh_ref, sc_ref, wr_ref, o_ref, aff_ref):
    h = _norm_body(x_ref, w_ref, sh_ref, sc_ref)
    o_ref[...] = h
    logits = lax.dot_general(wr_ref[...], h.astype(BF16), _NT, preferred_element_type=F32)
    e = jnp.exp(logits - jnp.max(logits, axis=0, keepdims=True))
    aff_ref[...] = e / jnp.sum(e, axis=0, keepdims=True)


def _norm_mod(x, w, shift, scale, n_tok, out_dtype=BF16, router_t=None):
    m, d = x.shape
    tm = _pick(n_tok, (256, 128))
    tiles = n_tok // tm
    per_sample = shift.shape[0] > 1
    mod_idx = (lambda i: (i // tiles, 0, 0)) if per_sample else (lambda i: (0, 0, 0))
    in_specs = [pl.BlockSpec((tm, d), lambda i: (i, 0)),
                pl.BlockSpec((1, d), lambda i: (0, 0)),
                pl.BlockSpec((None, 1, d), mod_idx),
                pl.BlockSpec((None, 1, d), mod_idx)]
    args = [x, w.reshape(1, d), shift, scale]
    o_spec = pl.BlockSpec((tm, d), lambda i: (i, 0))
    if router_t is None:
        return pl.pallas_call(
            _norm_kernel, grid=(m // tm,), in_specs=in_specs, out_specs=o_spec,
            out_shape=jax.ShapeDtypeStruct((m, d), out_dtype),
            compiler_params=_params(("arbitrary",)), name="norm_mod")(*args)
    n_exp = router_t.shape[0]
    return pl.pallas_call(
        _norm_router_kernel, grid=(m // tm,),
        in_specs=in_specs + [pl.BlockSpec((n_exp, d), lambda i: (0, 0))],
        out_specs=[o_spec, pl.BlockSpec((None, n_exp, tm), lambda i: (i // tiles, 0, i % tiles))],
        out_shape=[jax.ShapeDtypeStruct((m, d), F32),
                   jax.ShapeDtypeStruct((m // n_tok, n_exp, n_tok), F32)],
        compiler_params=_params(("arbitrary",)), name="norm_mod_router")(*args, router_t)


def _cast_weight_once(w_ref, wb_scr):
    @pl.when(pl.program_id(1) == 0)
    def _():
        wb_scr[...] = w_ref[...].astype(BF16)


def _proj_kernel(a_ref, w_ref, o_ref, wb_scr):
    _cast_weight_once(w_ref, wb_scr)
    o_ref[...] = jnp.dot(a_ref[...], wb_scr[...], preferred_element_type=F32).astype(o_ref.dtype)


def _proj_res_kernel(*refs):
    *a_refs, w_ref, r_ref, g_ref, o_ref, wb_scr = refs
    _cast_weight_once(w_ref, wb_scr)
    acc, k0 = None, 0
    for a_ref in a_refs:
        k1 = k0 + a_ref.shape[1]
        part = jnp.dot(a_ref[...], wb_scr[k0:k1, :], preferred_element_type=F32)
        acc, k0 = part if acc is None else acc + part, k1
    o_ref[...] = r_ref[...] + g_ref[...] * acc


def _mm_bias_kernel(a_ref, w_ref, b_ref, o_ref):
    o_ref[...] = jnp.dot(a_ref[...], w_ref[...], preferred_element_type=F32) + b_ref[...]


def _proj(a, w_stack, layer, n_cols, out_dtype=BF16):
    m, k = a.shape
    tm = _pick(m, (1024, 512, 256, 128))
    tn = _pick(n_cols, (512, 256, 128))
    return pl.pallas_call(
        _proj_kernel, grid=(n_cols // tn, m // tm),
        in_specs=[pl.BlockSpec((tm, k), lambda j, i: (i, 0)),
                  pl.BlockSpec((None, k, tn), lambda j, i: (layer, 0, j))],
        out_specs=pl.BlockSpec((tm, tn), lambda j, i: (i, j)),
        out_shape=jax.ShapeDtypeStruct((m, n_cols), out_dtype),
        scratch_shapes=[pltpu.VMEM((k, tn), BF16)],
        compiler_params=_params(("arbitrary", "arbitrary")), name="proj")(a, w_stack)


def _proj_residual(a_parts, w_stack, layer, res, gate, n_tok, in_place=True):
    m = a_parts[0].shape[0]
    _, k, n = w_stack.shape
    assert sum(a.shape[1] for a in a_parts) == k
    tm = _pick(n_tok, (1024, 512, 256, 128))
    tn = _pick(n, (512, 256, 128))
    tiles = n_tok // tm
    g_idx = (lambda j, i: (i // tiles, 0, j)) if gate.shape[0] > 1 else (lambda j, i: (0, 0, j))
    return pl.pallas_call(
        _proj_res_kernel, grid=(n // tn, m // tm),
        in_specs=[pl.BlockSpec((tm, a.shape[1]), lambda j, i: (i, 0)) for a in a_parts]
                 + [pl.BlockSpec((None, k, tn), lambda j, i: (layer, 0, j)),
                    pl.BlockSpec((tm, tn), lambda j, i: (i, j)),
                    pl.BlockSpec((None, 1, tn), g_idx)],
        out_specs=pl.BlockSpec((tm, tn), lambda j, i: (i, j)),
        out_shape=jax.ShapeDtypeStruct((m, n), F32),
        scratch_shapes=[pltpu.VMEM((k, tn), BF16)],
        input_output_aliases={len(a_parts) + 1: 0} if in_place else {},
        compiler_params=_params(("arbitrary", "arbitrary")), name="proj_residual")(*a_parts, w_stack, res, gate)


def _gate_proj(a, w, bias):
    m, k = a.shape
    n = w.shape[1]
    tm = _pick(m, (1024, 512, 256, 128))
    return pl.pallas_call(
        _mm_bias_kernel, grid=(m // tm,),
        in_specs=[pl.BlockSpec((tm, k), lambda i: (i, 0)), pl.BlockSpec((k, n), lambda i: (0, 0)),
                  pl.BlockSpec((1, n), lambda i: (0, 0))],
        out_specs=pl.BlockSpec((tm, n), lambda i: (i, 0)),
        out_shape=jax.ShapeDtypeStruct((m, n), F32),
        compiler_params=_params(("arbitrary",)), name="gate_proj")(a, w, bias.reshape(1, n))


def _na_bias_pairs(rpb):
    cols = np.arange(GRID_W)
    cstart = np.clip(cols - NA_COLS // 2, 0, GRID_W - NA_COLS)
    valid = (cols[None, :] >= cstart[:, None]) & (cols[None, :] < cstart[:, None] + NA_COLS)
    dc = cols[None, :] - cols[:, None] + NA_COLS - 1
    shift = (np.arange(2 * NA_COLS - 1)[:, None, None] == dc[None]) & valid[None]
    t = jnp.einsum("hrd,dqk->hrqk", rpb.astype(F32), jnp.asarray(shift, F32), precision=lax.Precision.HIGHEST)
    t = jnp.where(valid[None, None], t, NEG_INF)
    t = jnp.pad(t, ((0, 0), (1, 1), (0, 0), (0, 0)), constant_values=NEG_INF)
    return jnp.concatenate([t[:, :-1], t[:, 1:]], axis=-1)


def _na_kernel(q_ref, k_ref, v_ref, kc_ref, vc_ref, bias_ref, o_ref, *, rb, rows, kh, scale):
    gw, br = GRID_W, NA_BLOCK_ROWS
    wr = kh + br
    r0 = pl.program_id(2) * rb
    sc = lax.dot_general(q_ref[...], kc_ref[...], _NT, preferred_element_type=F32) * scale
    mc = jnp.max(sc, axis=-1, keepdims=True)
    mx_blocks, den_blocks, o_blocks = [], [], []
    for bi in range(rb // br):
        r4 = r0 + bi * br
        start = jnp.clip(r4 - kh // 2, 0, rows - wr)
        q = q_ref[bi * br * gw:(bi + 1) * br * gw, :]
        win = pl.ds(pl.multiple_of(start * gw, gw), wr * gw)
        bias = jnp.concatenate(
            [jnp.concatenate([bias_ref[jnp.clip(start - (r4 + a) + NA_ROWS + 2 * p, 0, 2 * NA_ROWS - 1)]
                              for p in range(wr // 2)], axis=1) for a in range(br)], axis=0)
        q_row = r4 + lax.broadcasted_iota(jnp.int32, (br * gw, 1), 0) // gw
        k_row = start + lax.broadcasted_iota(jnp.int32, (1, wr * gw), 1) // gw
        off = k_row - jnp.clip(q_row - kh // 2, 0, rows - kh)
        s = lax.dot_general(q, k_ref[win, :], _NT, preferred_element_type=F32) * scale + bias
        s = jnp.where(off >= 0, jnp.where(off < kh, s, NEG_INF), NEG_INF)
        mx = jnp.maximum(jnp.max(s, axis=-1, keepdims=True), mc[bi * br * gw:(bi + 1) * br * gw])
        p = jnp.exp(s - mx)
        mx_blocks.append(mx)
        den_blocks.append(jnp.sum(p, axis=-1, keepdims=True))
        o_blocks.append(jnp.dot(p.astype(BF16), v_ref[win, :], preferred_element_type=F32))
    pc = jnp.exp(sc - jnp.concatenate(mx_blocks, axis=0))
    den = jnp.concatenate(den_blocks, axis=0) + jnp.sum(pc, axis=-1, keepdims=True)
    o = jnp.concatenate(o_blocks, axis=0) + jnp.dot(pc.astype(BF16), vc_ref[...], preferred_element_type=F32)
    o_ref[...] = (o / den).astype(o_ref.dtype)


def _na_attention(qkv_l, qkv_c, bias, batch, n, lc, n_heads):
    rows = n // GRID_W
    kh = min(NA_ROWS, rows)
    assert kh % 2 == 0 and NA_BLOCK_ROWS % 2 == 0 and rows >= kh + NA_BLOCK_ROWS
    rb = _pick(rows, (2 * NA_BLOCK_ROWS, NA_BLOCK_ROWS))
    nrb = rows // rb
    hd = HEAD_DIM
    kern = functools.partial(_na_kernel, rb=rb, rows=rows, kh=kh, scale=hd ** -0.5)
    return pl.pallas_call(
        kern, grid=(batch, n_heads, nrb),
        in_specs=[pl.BlockSpec((rb * GRID_W, hd), lambda b, h, r: (b * nrb + r, h)),
                  pl.BlockSpec((n, hd), lambda b, h, r: (b, n_heads + h)),
                  pl.BlockSpec((n, hd), lambda b, h, r: (b, 2 * n_heads + h)),
                  pl.BlockSpec((lc, hd), lambda b, h, r: (b, n_heads + h)),
                  pl.BlockSpec((lc, hd), lambda b, h, r: (b, 2 * n_heads + h)),
                  pl.BlockSpec((None, 2 * NA_ROWS, GRID_W, 2 * GRID_W), lambda b, h, r: (h, 0, 0, 0))],
        out_specs=pl.BlockSpec((rb * GRID_W, hd), lambda b, h, r: (b * nrb + r, h)),
        out_shape=jax.ShapeDtypeStruct((batch * n, n_heads * hd), BF16),
        compiler_params=_params(("arbitrary", "arbitrary", "arbitrary")), name="na_attention",
    )(qkv_l, qkv_l, qkv_l, qkv_c, qkv_c, bias)


def _rope_tables(n):
    nf = HEAD_DIM // 4
    t = jnp.arange(n)
    inv = ROPE_BASE ** (-jnp.arange(nf, dtype=F32) / nf)
    ang_r = (t // GRID_W).astype(F32)[:, None] * inv
    ang_c = (t % GRID_W).astype(F32)[:, None] * inv
    cos = jnp.concatenate([jnp.cos(ang_r), jnp.cos(ang_r), jnp.cos(ang_c), jnp.cos(ang_c)], axis=-1)
    sin = jnp.concatenate([-jnp.sin(ang_r), jnp.sin(ang_r), -jnp.sin(ang_c), jnp.sin(ang_c)], axis=-1)
    return cos, sin


def _rope(x, cos, sin_signed):
    nf = HEAD_DIM // 4
    lane = lax.broadcasted_iota(jnp.int32, x.shape, 1)
    first = (lane % (2 * nf)) < nf
    swapped = jnp.where(first, pltpu.roll(x, HEAD_DIM - nf, 1), pltpu.roll(x, nf, 1))
    return x * cos + swapped * sin_signed


def _sink_column(sink_ref, base, g, rows_per_head):
    return jnp.concatenate([jnp.full((rows_per_head, 1), sink_ref[base + j], F32) for j in range(g)], axis=0)


def _win_kernel(sink_ref, q_ref, k_ref, v_ref, kc_ref, vc_ref, cq_ref, sq_ref, ck_ref, sk_ref, o_ref,
                *, g, n, scale, blocks):
    for u in range(blocks):
        rows = pl.ds(u * WIN_BLOCK, WIN_BLOCK)
        _win_block(sink_ref, q_ref.at[rows], k_ref, v_ref, kc_ref, vc_ref, cq_ref.at[rows], sq_ref.at[rows],
                   ck_ref, sk_ref, o_ref.at[rows], pl.program_id(2) * blocks + u, g=g, n=n, scale=scale)


def _win_block(sink_ref, q_ref, k_ref, v_ref, kc_ref, vc_ref, cq_ref, sq_ref, ck_ref, sk_ref, o_ref, blk,
               *, g, n, scale):
    hd = HEAD_DIM
    kvh = pl.program_id(1)
    span = WIN_BLOCK + 2 * WINDOW
    s0 = pl.multiple_of(jnp.clip(blk * WIN_BLOCK - WINDOW, 0, n - span), LANES)
    cq = cq_ref[...]
    sq = sq_ref[...]
    q = q_ref[...]
    qs = jnp.concatenate([_rope(q[:, j * hd:(j + 1) * hd].astype(F32), cq, sq) for j in range(g)],
                         axis=0).astype(BF16)
    kw = _rope(k_ref[pl.ds(s0, span), :].astype(F32), ck_ref[pl.ds(s0, span), :],
               sk_ref[pl.ds(s0, span), :]).astype(BF16)
    vw = v_ref[pl.ds(s0, span), :]
    s = lax.dot_general(qs, kw, _NT, preferred_element_type=F32) * scale
    qpos = blk * WIN_BLOCK + lax.broadcasted_iota(jnp.int32, (g * WIN_BLOCK, 1), 0) % WIN_BLOCK
    kpos = s0 + lax.broadcasted_iota(jnp.int32, (1, span), 1)
    s = jnp.where(jnp.abs(qpos - kpos) <= WINDOW, s, NEG_INF)
    sc = lax.dot_general(qs, kc_ref[...], _NT, preferred_element_type=F32) * scale
    sink = _sink_column(sink_ref, kvh * g, g, WIN_BLOCK)
    mx = jnp.maximum(jnp.maximum(jnp.max(s, axis=-1, keepdims=True), jnp.max(sc, axis=-1, keepdims=True)), sink)
    p = jnp.exp(s - mx)
    pc = jnp.exp(sc - mx)
    den = jnp.sum(p, axis=-1, keepdims=True) + jnp.sum(pc, axis=-1, keepdims=True) + jnp.exp(sink - mx)
    o = (jnp.dot(p.astype(BF16), vw, preferred_element_type=F32)
         + jnp.dot(pc.astype(BF16), vc_ref[...], preferred_element_type=F32)) / den
    o_ref[...] = jnp.concatenate([o[j * WIN_BLOCK:(j + 1) * WIN_BLOCK] for j in range(g)],
                                 axis=1).astype(o_ref.dtype)


def _win_attention(sink, qkv_l, qkv_c, cos, sin, batch, n, lc, n_na, n_q, n_kv):
    hd = HEAD_DIM
    g = n_q // n_kv
    blocks = _pick(n // WIN_BLOCK, (WIN_STEP_BLOCKS, 1))
    tq = WIN_BLOCK * blocks
    nb = n // tq
    q_blk0 = (3 * n_na * hd) // (g * hd)
    k_blk0 = 3 * n_na + n_q
    assert (3 * n_na) % g == 0
    kern = functools.partial(_win_kernel, g=g, n=n, scale=hd ** -0.5, blocks=blocks)
    full = lambda b, kv, i: (0, 0)
    return pl.pallas_call(
        kern, grid=(batch, n_kv, nb),
        in_specs=[pl.BlockSpec(memory_space=pltpu.SMEM),
                  pl.BlockSpec((tq, g * hd), lambda b, kv, i: (b * nb + i, q_blk0 + kv)),
                  pl.BlockSpec((n, hd), lambda b, kv, i: (b, k_blk0 + kv)),
                  pl.BlockSpec((n, hd), lambda b, kv, i: (b, k_blk0 + n_kv + kv)),
                  pl.BlockSpec((lc, hd), lambda b, kv, i: (b, k_blk0 + kv)),
                  pl.BlockSpec((lc, hd), lambda b, kv, i: (b, k_blk0 + n_kv + kv)),
                  pl.BlockSpec((tq, hd), lambda b, kv, i: (i, 0)),
                  pl.BlockSpec((tq, hd), lambda b, kv, i: (i, 0)),
                  pl.BlockSpec((n, hd), full),
                  pl.BlockSpec((n, hd), full)],
        out_specs=pl.BlockSpec((tq, g * hd), lambda b, kv, i: (b * nb + i, kv)),
        out_shape=jax.ShapeDtypeStruct((batch * n, n_q * hd), BF16),
        compiler_params=_params(("arbitrary", "arbitrary", "arbitrary")), name="window_attention",
    )(sink, qkv_l, qkv_l, qkv_l, qkv_c, qkv_c, cos, sin, cos, sin)


def _ctx_attn_kernel(sink_ref, q_ref, k_ref, v_ref, o_ref, *, g, lc, use_sink, scale):
    hd = HEAD_DIM
    q = q_ref[...]
    qs = jnp.concatenate([q[:, j * hd:(j + 1) * hd] for j in range(g)], axis=0)
    s = lax.dot_general(qs, k_ref[...], _NT, preferred_element_type=F32) * scale
    mx = jnp.max(s, axis=-1, keepdims=True)
    if use_sink:
        sink = _sink_column(sink_ref, pl.program_id(1) * g, g, lc)
        mx = jnp.maximum(mx, sink)
    p = jnp.exp(s - mx)
    den = jnp.sum(p, axis=-1, keepdims=True)
    if use_sink:
        den = den + jnp.exp(sink - mx)
    o = jnp.dot(p.astype(BF16), v_ref[...], preferred_element_type=F32) / den
    o_ref[...] = jnp.concatenate([o[j * lc:(j + 1) * lc] for j in range(g)], axis=1).astype(o_ref.dtype)


def _ctx_attention(sink, qkv_c, batch, lc, n_groups, g, q_blk0, k_blk0, v_blk0, use_sink):
    hd = HEAD_DIM
    kern = functools.partial(_ctx_attn_kernel, g=g, lc=lc, use_sink=use_sink, scale=hd ** -0.5)
    return pl.pallas_call(
        kern, grid=(batch, n_groups),
        in_specs=[pl.BlockSpec(memory_space=pltpu.SMEM),
                  pl.BlockSpec((lc, g * hd), lambda b, h: (b, q_blk0 + h)),
                  pl.BlockSpec((lc, hd), lambda b, h: (b, k_blk0 + h)),
                  pl.BlockSpec((lc, hd), lambda b, h: (b, v_blk0 + h))],
        out_specs=pl.BlockSpec((lc, g * hd), lambda b, h: (b, h)),
        out_shape=jax.ShapeDtypeStruct((batch * lc, n_groups * g * hd), BF16),
        compiler_params=_params(("arbitrary", "arbitrary")), name="ctx_attention",
    )(sink, qkv_c, qkv_c, qkv_c)


def _log_sigmoid(x):
    return jnp.minimum(x, 0.0) - jnp.log(1.0 + jnp.exp(-jnp.abs(x)))


def _mlstm_kernel(*refs, chunk, dv, nchunks, reverse, i_off, f_off, fuse_out, qscale):
    if fuse_out:
        (q_ref, k_ref, v_ref, g_ref, c0_ref, m0_ref, hb_ref, og_ref, nw_ref,
         h_ref, ct_ref, mt_ref, c_scr, m_scr) = refs
    else:
        q_ref, k_ref, v_ref, g_ref, c0_ref, m0_ref, h_ref, ct_ref, mt_ref, c_scr, m_scr = refs
    head = pl.program_id(1)
    step = pl.program_id(2)

    @pl.when(step == 0)
    def _():
        c_scr[...] = c0_ref[...]
        m_scr[...] = m0_ref[...]

    gates = g_ref[...]
    lane = lax.broadcasted_iota(jnp.int32, gates.shape, 1)
    i_col = jnp.sum(jnp.where(lane == i_off + head, gates, 0.0), axis=1, keepdims=True)
    f_col = jnp.sum(jnp.where(lane == f_off + head, gates, 0.0), axis=1, keepdims=True)
    gates_t = gates.T
    sub = lax.broadcasted_iota(jnp.int32, gates_t.shape, 0)
    i_row = jnp.sum(jnp.where(sub == i_off + head, gates_t, 0.0), axis=0, keepdims=True)
    f_row = jnp.sum(jnp.where(sub == f_off + head, gates_t, 0.0), axis=0, keepdims=True)
    lf_col = _log_sigmoid(f_col)
    lf_row = _log_sigmoid(f_row)

    tt = lax.broadcasted_iota(jnp.int32, (chunk, chunk), 0)
    ss = lax.broadcasted_iota(jnp.int32, (chunk, chunk), 1)
    causal = (ss >= tt) if reverse else (ss <= tt)
    anti = (tt >= ss) if reverse else (tt <= ss)
    b_col = jnp.sum(jnp.where(causal, lf_row, 0.0), axis=1, keepdims=True)
    b_row = jnp.sum(jnp.where(anti, lf_col, 0.0), axis=0, keepdims=True)
    g_tot = jnp.sum(lf_row, axis=1, keepdims=True)

    m_old = m_scr[:, 0:1]
    a_row = g_tot - b_row + i_row
    a_col = g_tot - b_col + i_col
    m_new = jnp.maximum(g_tot + m_old, jnp.max(a_row, axis=1, keepdims=True))
    wk_col = jnp.exp(a_col - m_new)
    decay = jnp.exp(g_tot + m_old - m_new)

    q = q_ref[...]
    k = k_ref[...]
    v_aug = jnp.concatenate([v_ref[...], jnp.ones((chunk, LANES), BF16)], axis=1)
    c_old = c_scr[...]

    inter = jnp.dot(q, c_old.astype(BF16), preferred_element_type=F32) * qscale
    qk = lax.dot_general(q, k, _NT, preferred_element_type=F32) * qscale
    dmat = jnp.where(causal, b_col - b_row + i_row, -jnp.inf)
    m_inter = b_col + m_old
    m_t = jnp.maximum(m_inter, jnp.max(dmat, axis=1, keepdims=True))
    w_inter = jnp.exp(m_inter - m_t)
    pw = jnp.exp(dmat - m_t) * qk
    num = w_inter * inter + jnp.dot(pw.astype(BF16), v_aug, preferred_element_type=F32)
    den = num[:, dv:dv + 1]
    hc = num[:, :dv] / jnp.maximum(jnp.abs(den), jnp.exp(-m_t))

    kw = (k.astype(F32) * wk_col).astype(BF16)
    c_scr[...] = decay * c_old + lax.dot_general(kw, v_aug, _TN, preferred_element_type=F32)
    m_scr[...] = jnp.broadcast_to(m_new, m_scr.shape)

    if fuse_out:
        hs = hc + hb_ref[...]
        hn = hs * lax.rsqrt(jnp.mean(hs * hs, axis=-1, keepdims=True) + NORM_EPS) * nw_ref[...]
        h_ref[...] = (hn * jax.nn.sigmoid(og_ref[...].astype(F32))).astype(h_ref.dtype)
    else:
        h_ref[...] = hc

    @pl.when(step == nchunks - 1)
    def _():
        ct_ref[...] = c_scr[...]
        mt_ref[...] = m_scr[...]


def _mlstm_scan(proj, gates, state, n_tok, reverse, fuse=None):
    nh = MLSTM_HEADS
    m = proj.shape[0]
    dv = proj.shape[1] // (3 * nh)
    dqk = dv // 2
    batch = m // n_tok
    chunk = min(MLSTM_KERNEL_CHUNK, n_tok)
    nchunks = n_tok // chunk
    c0, m0 = state
    cidx = (lambda c: nchunks - 1 - c) if reverse else (lambda c: c)
    row = lambda b, c: b * nchunks + cidx(c)
    i_off, f_off = (2 * nh, 3 * nh) if reverse else (0, nh)
    in_specs = [pl.BlockSpec((chunk, dqk), lambda b, h, c: (row(b, c), h)),
                pl.BlockSpec((chunk, dqk), lambda b, h, c: (row(b, c), nh + h)),
                pl.BlockSpec((chunk, dv), lambda b, h, c: (row(b, c), nh + h)),
                pl.BlockSpec((chunk, LANES), lambda b, h, c: (row(b, c), 0)),
                pl.BlockSpec((None, None, dqk, dv + LANES), lambda b, h, c: (b, h, 0, 0)),
                pl.BlockSpec((None, None, 1, LANES), lambda b, h, c: (b, h, 0, 0))]
    args = [proj, proj, proj, gates, c0, m0]
    if fuse is not None:
        h_other, norm_w = fuse
        in_specs += [pl.BlockSpec((chunk, dv), lambda b, h, c: (row(b, c), h)),
                     pl.BlockSpec((chunk, dv), lambda b, h, c: (row(b, c), 2 * nh + h)),
                     pl.BlockSpec((1, dv), lambda b, h, c: (0, h))]
        args += [h_other, proj, norm_w.reshape(1, nh * dv)]
    kern = functools.partial(_mlstm_kernel, chunk=chunk, dv=dv, nchunks=nchunks, reverse=reverse,
                             i_off=i_off, f_off=f_off, fuse_out=fuse is not None, qscale=dqk ** -0.5)
    return pl.pallas_call(
        kern, grid=(batch, nh, nchunks), in_specs=in_specs,
        out_specs=[pl.BlockSpec((chunk, dv), lambda b, h, c: (row(b, c), h)),
                   pl.BlockSpec((None, None, dqk, dv + LANES), lambda b, h, c: (b, h, 0, 0)),
                   pl.BlockSpec((None, None, 1, LANES), lambda b, h, c: (b, h, 0, 0))],
        out_shape=[jax.ShapeDtypeStruct((m, nh * dv), F32 if fuse is None else BF16),
                   jax.ShapeDtypeStruct(c0.shape, F32),
                   jax.ShapeDtypeStruct(m0.shape, F32)],
        scratch_shapes=[pltpu.VMEM((dqk, dv + LANES), F32), pltpu.VMEM((1, LANES), F32)],
        compiler_params=_params(("arbitrary", "arbitrary", "arbitrary")),
        name="mlstm_scan_bwd" if reverse else "mlstm_scan_fwd")(*args)


def _excl_prefix(mask_f, upper):
    n_exp, t = mask_f.shape
    off = jnp.zeros((n_exp, 1), F32)
    parts = []
    for j in range(t // LANES):
        blk = mask_f[:, j * LANES:(j + 1) * LANES]
        parts.append(jnp.dot(blk.astype(BF16), upper, preferred_element_type=F32) + off)
        off = off + jnp.sum(blk, axis=1, keepdims=True)
    return jnp.concatenate(parts, axis=1)


def _select_kernel(aff_ref, idx_ref, gate_ref, slot_scr, *, cap, n_tok):
    e = pl.program_id(1)
    n_exp, t = aff_ref.shape

    @pl.when(e == 0)
    def _():
        a = aff_ref[...]
        bits = pltpu.bitcast(a, jnp.int32)
        thr = jnp.zeros((n_exp, 1), jnp.int32)
        for bit in range(30, -1, -1):
            cand = thr | (1 << bit)
            cnt = jnp.sum(jnp.where(bits >= cand, 1.0, 0.0), axis=1, keepdims=True)
            thr = jnp.where(cnt >= cap, cand, thr)
        uu = lax.broadcasted_iota(jnp.int32, (LANES, LANES), 0)
        vv = lax.broadcasted_iota(jnp.int32, (LANES, LANES), 1)
        upper = jnp.where(uu < vv, 1.0, 0.0).astype(BF16)
        gt = jnp.where(bits > thr, 1.0, 0.0)
        eq = jnp.where(bits == thr, 1.0, 0.0)
        need = cap - jnp.sum(gt, axis=1, keepdims=True)
        sel = gt + eq * jnp.where(_excl_prefix(eq, upper) < need, 1.0, 0.0)
        slot_scr[...] = jnp.where(sel > 0.5, _excl_prefix(sel, upper), -1.0)

    a = aff_ref[pl.ds(e, 1), :]
    slot = slot_scr[pl.ds(e, 1), :]
    a_hi = a.astype(BF16).astype(F32)
    a_mid = (a - a_hi).astype(BF16).astype(F32)
    a_lo = a - a_hi - a_mid
    tpos = lax.broadcasted_iota(jnp.int32, (1, t), 1)
    t_hi = (tpos >> 6).astype(F32)
    t_lo = (tpos & 63).astype(F32)
    row = lax.broadcasted_iota(jnp.int32, (16, t), 0)
    lhs = jnp.where(row == 0, t_hi, jnp.where(row == 1, t_lo, jnp.where(row == 2, a_hi, jnp.where(
        row == 3, a_mid, jnp.where(row == 4, a_lo, 0.0))))).astype(BF16)
    s_iota = lax.broadcasted_iota(jnp.int32, (cap, t), 0).astype(F32)
    onehot = jnp.where(slot == s_iota, 1.0, 0.0).astype(BF16)
    r = lax.dot_general(lhs, onehot, _NT, preferred_element_type=F32)
    idx_ref[...] = (r[0:1] * 64.0 + r[1:2]).astype(jnp.int32) + pl.program_id(0) * n_tok
    gate_ref[...] = r[2:3] + r[3:4] + r[4:5]


def _select(aff, cap):
    batch, n_exp, n_tok = aff.shape
    kern = functools.partial(_select_kernel, cap=cap, n_tok=n_tok)
    out_spec = pl.BlockSpec((None, None, 1, cap), lambda b, e: (b, e, 0, 0))
    return pl.pallas_call(
        kern, grid=(batch, n_exp),
        in_specs=[pl.BlockSpec((None, n_exp, n_tok), lambda b, e: (b, 0, 0))],
        out_specs=[out_spec, out_spec],
        out_shape=[jax.ShapeDtypeStruct((batch, n_exp, 1, cap), jnp.int32),
                   jax.ShapeDtypeStruct((batch, n_exp, 1, cap), F32)],
        scratch_shapes=[pltpu.VMEM((n_exp, n_tok), F32)],
        compiler_params=_params(("arbitrary", "arbitrary")), name="moe_select")(aff)


def _moe_ffn_kernel(idx_ref, h_hbm, x_hbm, w1_ref, w3_ref, w2_ref, gate_ref, g2_ref, o_hbm,
                    h_buf, x_buf, sems, *, cap, n_exp):
    base = (pl.program_id(1) * n_exp + pl.program_id(0)) * cap

    def h_copy(s):
        return pltpu.make_async_copy(h_hbm.at[pl.ds(idx_ref[base + s], 1)], h_buf.at[pl.ds(s, 1)], sems.at[0])

    def x_copy(s):
        return pltpu.make_async_copy(x_hbm.at[pl.ds(idx_ref[base + s], 1)], x_buf.at[pl.ds(s, 1)], sems.at[1])

    def o_copy(s):
        return pltpu.make_async_copy(x_buf.at[pl.ds(s, 1)], o_hbm.at[pl.ds(idx_ref[base + s], 1)], sems.at[2])

    def start_in(s, carry):
        h_copy(s).start()
        x_copy(s).start()
        return carry

    def wait_in(s, carry):
        h_copy(s).wait()
        x_copy(s).wait()
        return carry

    def start_out(s, carry):
        o_copy(s).start()
        return carry

    def wait_out(s, carry):
        o_copy(s).wait()
        return carry

    lax.fori_loop(0, cap, start_in, 0, unroll=DMA_UNROLL)
    lax.fori_loop(0, cap, wait_in, 0, unroll=DMA_UNROLL)
    x = h_buf[...].astype(BF16)
    a = jnp.dot(x, w1_ref[...], preferred_element_type=F32)
    u = jnp.dot(x, w3_ref[...], preferred_element_type=F32)
    hmid = (a * jax.nn.sigmoid(a) * u).astype(BF16)
    y = jnp.dot(hmid, w2_ref[...], preferred_element_type=F32)
    ii = lax.broadcasted_iota(jnp.int32, (cap, cap), 0)
    jj = lax.broadcasted_iota(jnp.int32, (cap, cap), 1)
    gate_col = jnp.sum(jnp.where(ii == jj, gate_ref[...], 0.0), axis=1, keepdims=True)
    x_buf[...] = x_buf[...] + g2_ref[...] * (y * gate_col)
    lax.fori_loop(0, cap, start_out, 0, unroll=DMA_UNROLL)
    lax.fori_loop(0, cap, wait_out, 0, unroll=DMA_UNROLL)


def _moe_ffn(idx, gate, h2, x, w1, w3, w2, gate2):
    batch, n_exp, _, cap = idx.shape
    m, d = x.shape
    ff = w1.shape[-1]
    g_idx = (lambda e, b, i: (b, 0, 0)) if gate2.shape[0] > 1 else (lambda e, b, i: (0, 0, 0))
    kern = functools.partial(_moe_ffn_kernel, cap=cap, n_exp=n_exp)
    grid_spec = pltpu.PrefetchScalarGridSpec(
        num_scalar_prefetch=1, grid=(n_exp, batch),
        in_specs=[pl.BlockSpec(memory_space=pl.ANY),
                  pl.BlockSpec(memory_space=pl.ANY),
                  pl.BlockSpec((None, d, ff), lambda e, b, i: (e, 0, 0)),
                  pl.BlockSpec((None, d, ff), lambda e, b, i: (e, 0, 0)),
                  pl.BlockSpec((None, ff, d), lambda e, b, i: (e, 0, 0)),
                  pl.BlockSpec((None, None, 1, cap), lambda e, b, i: (b, e, 0, 0)),
                  pl.BlockSpec((None, 1, d), g_idx)],
        out_specs=pl.BlockSpec(memory_space=pl.ANY),
        scratch_shapes=[pltpu.VMEM((cap, d), F32), pltpu.VMEM((cap, d), F32), pltpu.SemaphoreType.DMA((3,))])
    return pl.pallas_call(
        kern, grid_spec=grid_spec,
        out_shape=jax.ShapeDtypeStruct((m, d), F32),
        input_output_aliases={2: 0},
        compiler_params=_params(("arbitrary", "arbitrary")), name="moe_ffn",
    )(idx.reshape(-1), h2, x, w1, w3, w2, gate, gate2)


def _moe(x, norm_w, shift, scale, gate2, router_t, w1, w3, w2, n_tok):
    n_exp = router_t.shape[0]
    cap = (CAPACITY_FACTOR * n_tok) // n_exp
    h2, aff = _norm_mod(x, norm_w, shift, scale, n_tok, router_t=router_t)
    idx, gate = _select(aff, cap)
    return _moe_ffn(idx, gate, h2, x, w1, w3, w2, gate2)


def kernel(x, c, ctx, c_ctx, ada_w, ada_b, norm1_w, norm2_w, ab_w_in, ab_w_out, na_rpb, win_sink,
           ml_w_in, ml_b_gates, ml_norm_w, ml_w_out, moe_router, moe_w1, moe_w3, moe_w2, final_norm_w):
    batch, n, d = x.shape
    lc = ctx.shape[1]
    depth = ada_w.shape[0]
    hd = HEAD_DIM
    n_heads = d // hd
    n_na = n_heads // 2
    n_q = n_heads - n_na
    n_kv = max(1, n_q // 4)
    nh = MLSTM_HEADS
    dv = d // nh
    dqk = dv // 2
    assert batch + 1 <= 8

    xl = x.reshape(batch * n, d)
    xc = ctx.reshape(batch * lc, d)
    cc = jnp.concatenate([c, c_ctx[None], jnp.zeros((8 - batch - 1, d), F32)], axis=0)
    mods = _ada_mods(cc, ada_w, ada_b)
    cos, sin = _rope_tables(n)

    for l in range(depth):
        need_ctx = l < depth - 1
        ml = mods[l].reshape(8, ADA_MODS, 1, d)
        sh1, sc1, g1, sh2, sc2, g2 = (ml[:batch, j] for j in range(ADA_MODS))
        ch1, cs1, cg1, ch2, cs2, cg2 = (ml[batch:batch + 1, j] for j in range(ADA_MODS))
        hl = _norm_mod(xl, norm1_w[l], sh1, sc1, n)
        hc = _norm_mod(xc, norm1_w[l], ch1, cs1, lc)
        if l % 2 == 0:
            e = l // 2
            qkv_l = _proj(hl, ab_w_in, e, ab_w_in.shape[2])
            qkv_c = _proj(hc, ab_w_in, e, ab_w_in.shape[2])
            bias = _na_bias_pairs(na_rpb[e])
            na_l = _na_attention(qkv_l, qkv_c, bias, batch, n, lc, n_na)
            win_l = _win_attention(win_sink[e], qkv_l, qkv_c, cos, sin, batch, n, lc, n_na, n_q, n_kv)
            xl = _proj_residual([na_l, win_l], ab_w_out, e, xl, g1, n, in_place=l > 0)
            if need_ctx:
                g = n_q // n_kv
                na_c = _ctx_attention(win_sink[e], qkv_c, batch, lc, n_na, 1, 0, n_na, 2 * n_na, False)
                win_c = _ctx_attention(win_sink[e], qkv_c, batch, lc, n_kv, g,
                                       (3 * n_na) // g, 3 * n_na + n_q, 3 * n_na + n_q + n_kv, True)
                xc = _proj_residual([na_c, win_c], ab_w_out, e, xc, cg1, lc, in_place=l > 0)
        else:
            o = l // 2
            main_w = 2 * nh * dqk + 2 * nh * dv
            n_gates = ml_w_in.shape[2] - main_w
            w_gate = jnp.pad(ml_w_in[o, :, main_w:], ((0, 0), (0, LANES - n_gates))).astype(BF16)
            b_gate = jnp.pad(ml_b_gates[o], (0, LANES - n_gates))
            pl_l, pl_c = _proj(hl, ml_w_in, o, main_w), _proj(hc, ml_w_in, o, main_w)
            gt_l = _gate_proj(hl, w_gate, b_gate)
            gt_c = _gate_proj(hc, w_gate, b_gate)
            state0 = (jnp.zeros((batch, nh, dqk, dv + LANES), F32), jnp.zeros((batch, nh, 1, LANES), F32))
            hcb, cb, mb = _mlstm_scan(pl_c, gt_c, state0, lc, True)
            hlb, _, _ = _mlstm_scan(pl_l, gt_l, (cb, mb), n, True)
            yc, cf, mf = _mlstm_scan(pl_c, gt_c, state0, lc, False, fuse=(hcb, ml_norm_w[o]))
            yl, _, _ = _mlstm_scan(pl_l, gt_l, (cf, mf), n, False, fuse=(hlb, ml_norm_w[o]))
            xl = _proj_residual([yl], ml_w_out, o, xl, g1, n, in_place=l > 0)
            if need_ctx:
                xc = _proj_residual([yc], ml_w_out, o, xc, cg1, lc, in_place=l > 0)
        router_t = moe_router[l].T.astype(BF16)
        w1, w3, w2 = moe_w1[l].astype(BF16), moe_w3[l].astype(BF16), moe_w2[l].astype(BF16)
        xl = _moe(xl, norm2_w[l], sh2, sc2, g2, router_t, w1, w3, w2, n)
        if need_ctx:
            xc = _moe(xc, norm2_w[l], ch2, cs2, cg2, router_t, w1, w3, w2, lc)

    zeros = jnp.zeros((1, 1, d), F32)
    out = _norm_mod(xl, final_norm_w, zeros, zeros, n, out_dtype=F32)
    return out.reshape(batch, n, d)
```

```python
import functools

import numpy as np
import jax
import jax.numpy as jnp
from jax import lax
from jax.experimental import pallas as pl
from jax.experimental.pallas import tpu as pltpu

GRID_W = 64
HEAD_DIM = 128
NA_ROWS = 8
NA_COLS = 16
NA_BLOCK_ROWS = 4
WINDOW = 128
WIN_BLOCK = 128
WIN_STEP_BLOCKS = 2
ROPE_BASE = 10000.0
MLSTM_HEADS = 8
MLSTM_KERNEL_CHUNK = 256
CAPACITY_FACTOR = 2
ADA_MODS = 6
NORM_EPS = 1e-6
NEG_INF = -1e30
LANES = 128
MOE_ROW_TILE = 256
DMA_UNROLL = 8
V7X_VMEM_LIMIT_BYTES = 56 * 1024 * 1024

F32 = jnp.float32
BF16 = jnp.bfloat16
_NT = (((1,), (1,)), ((), ()))
_TN = (((0,), (0,)), ((), ()))


def _pick(n, cands):
    for c in cands:
        if n % c == 0:
            return c
    raise ValueError(f"no tile in {cands} divides {n}")


def _params(sem):
    return pltpu.CompilerParams(dimension_semantics=sem, vmem_limit_bytes=V7X_VMEM_LIMIT_BYTES)


def _ada_kernel(cc_ref, w_ref, b_ref, o_ref):
    x = cc_ref[...]
    s = x * jax.nn.sigmoid(x)
    o_ref[...] = jnp.dot(s.astype(BF16), w_ref[...].astype(BF16), preferred_element_type=F32) + b_ref[...]


def _ada_mods(cc, ada_w, ada_b):
    depth, d, n = ada_w.shape
    tn = _pick(n, (512, 256, 128))
    return pl.pallas_call(
        _ada_kernel,
        grid=(depth, n // tn),
        in_specs=[pl.BlockSpec((8, d), lambda l, j: (0, 0)),
                  pl.BlockSpec((None, d, tn), lambda l, j: (l, 0, j)),
                  pl.BlockSpec((None, 1, tn), lambda l, j: (l, 0, j))],
        out_specs=pl.BlockSpec((None, 8, tn), lambda l, j: (l, 0, j)),
        out_shape=jax.ShapeDtypeStruct((depth, 8, n), F32),
        compiler_params=_params(("arbitrary", "arbitrary")),
        name="ada_mods",
    )(cc, ada_w, ada_b.reshape(depth, 1, n))


def _norm_body(x_ref, w_ref, sh_ref, sc_ref):
    x = x_ref[...]
    y = x * lax.rsqrt(jnp.mean(x * x, axis=-1, keepdims=True) + NORM_EPS) * w_ref[...]
    return y * (1.0 + sc_ref[...]) + sh_ref[...]


def _norm_kernel(x_ref, w_ref, sh_ref, sc_ref, o_ref):
    o_ref[...] = _norm_body(x_ref, w_ref, sh_ref, sc_ref).astype(o_ref.dtype)


def _norm_router_kernel(x_ref, w_ref, sh_ref, sc_ref, wr_ref, o_ref, aff_ref):
    h = _norm_body(x_ref, w_ref, sh_ref, sc_ref)
    o_ref[...] = h
    logits = lax.dot_general(wr_ref[...], h.astype(BF16), _NT, preferred_element_type=F32)
    e = jnp.exp(logits - jnp.max(logits, axis=0, keepdims=True))
    aff_ref[...] = e / jnp.sum(e, axis=0, keepdims=True)


def _norm_mod(x, w, shift, scale, n_tok, out_dtype=BF16, router_t=None):
    m, d = x.shape
    tm = _pick(n_tok, (256, 128))
    tiles = n_tok // tm
    per_sample = shift.shape[0] > 1
    mod_idx = (lambda i: (i // tiles, 0, 0)) if per_sample else (lambda i: (0, 0, 0))
    in_specs = [pl.BlockSpec((tm, d), lambda i: (i, 0)),
                pl.BlockSpec((1, d), lambda i: (0, 0)),
                pl.BlockSpec((None, 1, d), mod_idx),
                pl.BlockSpec((None, 1, d), mod_idx)]
    args = [x, w.reshape(1, d), shift, scale]
    o_spec = pl.BlockSpec((tm, d), lambda i: (i, 0))
    if router_t is None:
        return pl.pallas_call(
            _norm_kernel, grid=(m // tm,), in_specs=in_specs, out_specs=o_spec,
            out_shape=jax.ShapeDtypeStruct((m, d), out_dtype),
            compiler_params=_params(("arbitrary",)), name="norm_mod")(*args)
    n_exp = router_t.shape[0]
    return pl.pallas_call(
        _norm_router_kernel, grid=(m // tm,),
        in_specs=in_specs + [pl.BlockSpec((n_exp, d), lambda i: (0, 0))],
        out_specs=[o_spec, pl.BlockSpec((None, n_exp, tm), lambda i: (i // tiles, 0, i % tiles))],
        out_shape=[jax.ShapeDtypeStruct((m, d), F32),
                   jax.ShapeDtypeStruct((m // n_tok, n_exp, n_tok), F32)],
        compiler_params=_params(("arbitrary",)), name="norm_mod_router")(*args, router_t)


def _cast_weight_once(w_ref, wb_scr):
    @pl.when(pl.program_id(1) == 0)
    def _():
        wb_scr[...] = w_ref[...].astype(BF16)


def _proj_kernel(a_ref, w_ref, o_ref, wb_scr):
    _cast_weight_once(w_ref, wb_scr)
    o_ref[...] = jnp.dot(a_ref[...], wb_scr[...], preferred_element_type=F32).astype(o_ref.dtype)


def _proj_res_kernel(*refs):
    *a_refs, w_ref, r_ref, g_ref, o_ref, wb_scr = refs
    _cast_weight_once(w_ref, wb_scr)
    acc, k0 = None, 0
    for a_ref in a_refs:
        k1 = k0 + a_ref.shape[1]
        part = jnp.dot(a_ref[...], wb_scr[k0:k1, :], preferred_element_type=F32)
        acc, k0 = part if acc is None else acc + part, k1
    o_ref[...] = r_ref[...] + g_ref[...] * acc


def _mm_bias_kernel(a_ref, w_ref, b_ref, o_ref):
    o_ref[...] = jnp.dot(a_ref[...], w_ref[...], preferred_element_type=F32) + b_ref[...]


def _proj(a, w_stack, layer, n_cols, out_dtype=BF16):
    m, k = a.shape
    tm = _pick(m, (1024, 512, 256, 128))
    tn = _pick(n_cols, (512, 256, 128))
    return pl.pallas_call(
        _proj_kernel, grid=(n_cols // tn, m // tm),
        in_specs=[pl.BlockSpec((tm, k), lambda j, i: (i, 0)),
                  pl.BlockSpec((None, k, tn), lambda j, i: (layer, 0, j))],
        out_specs=pl.BlockSpec((tm, tn), lambda j, i: (i, j)),
        out_shape=jax.ShapeDtypeStruct((m, n_cols), out_dtype),
        scratch_shapes=[pltpu.VMEM((k, tn), BF16)],
        compiler_params=_params(("arbitrary", "arbitrary")), name="proj")(a, w_stack)


def _proj_residual(a_parts, w_stack, layer, res, gate, n_tok, in_place=True):
    m = a_parts[0].shape[0]
    _, k, n = w_stack.shape
    assert sum(a.shape[1] for a in a_parts) == k
    tm = _pick(n_tok, (1024, 512, 256, 128))
    tn = _pick(n, (512, 256, 128))
    tiles = n_tok // tm
    g_idx = (lambda j, i: (i // tiles, 0, j)) if gate.shape[0] > 1 else (lambda j, i: (0, 0, j))
    return pl.pallas_call(
        _proj_res_kernel, grid=(n // tn, m // tm),
        in_specs=[pl.BlockSpec((tm, a.shape[1]), lambda j, i: (i, 0)) for a in a_parts]
                 + [pl.BlockSpec((None, k, tn), lambda j, i: (layer, 0, j)),
                    pl.BlockSpec((tm, tn), lambda j, i: (i, j)),
                    pl.BlockSpec((None, 1, tn), g_idx)],
        out_specs=pl.BlockSpec((tm, tn), lambda j, i: (i, j)),
        out_shape=jax.ShapeDtypeStruct((m, n), F32),
        scratch_shapes=[pltpu.VMEM((k, tn), BF16)],
        input_output_aliases={len(a_parts) + 1: 0} if in_place else {},
        compiler_params=_params(("arbitrary", "arbitrary")), name="proj_residual")(*a_parts, w_stack, res, gate)


def _gate_proj(a, w, bias):
    m, k = a.shape
    n = w.shape[1]
    tm = _pick(m, (1024, 512, 256, 128))
    return pl.pallas_call(
        _mm_bias_kernel, grid=(m // tm,),
        in_specs=[pl.BlockSpec((tm, k), lambda i: (i, 0)), pl.BlockSpec((k, n), lambda i: (0, 0)),
                  pl.BlockSpec((1, n), lambda i: (0, 0))],
        out_specs=pl.BlockSpec((tm, n), lambda i: (i, 0)),
        out_shape=jax.ShapeDtypeStruct((m, n), F32),
        compiler_params=_params(("arbitrary",)), name="gate_proj")(a, w, bias.reshape(1, n))


def _na_bias_pairs(rpb):
    cols = np.arange(GRID_W)
    cstart = np.clip(cols - NA_COLS // 2, 0, GRID_W - NA_COLS)
    valid = (cols[None, :] >= cstart[:, None]) & (cols[None, :] < cstart[:, None] + NA_COLS)
    dc = cols[None, :] - cols[:, None] + NA_COLS - 1
    shift = (np.arange(2 * NA_COLS - 1)[:, None, None] == dc[None]) & valid[None]
    t = jnp.einsum("hrd,dqk->hrqk", rpb.astype(F32), jnp.asarray(shift, F32), precision=lax.Precision.HIGHEST)
    t = jnp.where(valid[None, None], t, NEG_INF)
    t = jnp.pad(t, ((0, 0), (1, 1), (0, 0), (0, 0)), constant_values=NEG_INF)
    return jnp.concatenate([t[:, :-1], t[:, 1:]], axis=-1)


def _na_kernel(q_ref, k_ref, v_ref, kc_ref, vc_ref, bias_ref, o_ref, *, rb, rows, kh, scale):
    gw, br = GRID_W, NA_BLOCK_ROWS
    wr = kh + br
    r0 = pl.program_id(2) * rb
    sc = lax.dot_general(q_ref[...], kc_ref[...], _NT, preferred_element_type=F32) * scale
    mc = jnp.max(sc, axis=-1, keepdims=True)
    mx_blocks, den_blocks, o_blocks = [], [], []
    for bi in range(rb // br):
        r4 = r0 + bi * br
        start = jnp.clip(r4 - kh // 2, 0, rows - wr)
        q = q_ref[bi * br * gw:(bi + 1) * br * gw, :]
        win = pl.ds(pl.multiple_of(start * gw, gw), wr * gw)
        bias = jnp.concatenate(
            [jnp.concatenate([bias_ref[jnp.clip(start - (r4 + a) + NA_ROWS + 2 * p, 0, 2 * NA_ROWS - 1)]
                              for p in range(wr // 2)], axis=1) for a in range(br)], axis=0)
        q_row = r4 + lax.broadcasted_iota(jnp.int32, (br * gw, 1), 0) // gw
        k_row = start + lax.broadcasted_iota(jnp.int32, (1, wr * gw), 1) // gw
        off = k_row - jnp.clip(q_row - kh // 2, 0, rows - kh)
        s = lax.dot_general(q, k_ref[win, :], _NT, preferred_element_type=F32) * scale + bias
        s = jnp.where(off >= 0, jnp.where(off < kh, s, NEG_INF), NEG_INF)
        mx = jnp.maximum(jnp.max(s, axis=-1, keepdims=True), mc[bi * br * gw:(bi + 1) * br * gw])
        p = jnp.exp(s - mx)
        mx_blocks.append(mx)
        den_blocks.append(jnp.sum(p, axis=-1, keepdims=True))
        o_blocks.append(jnp.dot(p.astype(BF16), v_ref[win, :], preferred_element_type=F32))
    pc = jnp.exp(sc - jnp.concatenate(mx_blocks, axis=0))
    den = jnp.concatenate(den_blocks, axis=0) + jnp.sum(pc, axis=-1, keepdims=True)
    o = jnp.concatenate(o_blocks, axis=0) + jnp.dot(pc.astype(BF16), vc_ref[...], preferred_element_type=F32)
    o_ref[...] = (o / den).astype(o_ref.dtype)


def _na_attention(qkv_l, qkv_c, bias, batch, n, lc, n_heads):
    rows = n // GRID_W
    kh = min(NA_ROWS, rows)
    assert kh % 2 == 0 and NA_BLOCK_ROWS % 2 == 0 and rows >= kh + NA_BLOCK_ROWS
    rb = _pick(rows, (2 * NA_BLOCK_ROWS, NA_BLOCK_ROWS))
    nrb = rows // rb
    hd = HEAD_DIM
    kern = functools.partial(_na_kernel, rb=rb, rows=rows, kh=kh, scale=hd ** -0.5)
    return pl.pallas_call(
        kern, grid=(batch, n_heads, nrb),
        in_specs=[pl.BlockSpec((rb * GRID_W, hd), lambda b, h, r: (b * nrb + r, h)),
                  pl.BlockSpec((n, hd), lambda b, h, r: (b, n_heads + h)),
                  pl.BlockSpec((n, hd), lambda b, h, r: (b, 2 * n_heads + h)),
                  pl.BlockSpec((lc, hd), lambda b, h, r: (b, n_heads + h)),
                  pl.BlockSpec((lc, hd), lambda b, h, r: (b, 2 * n_heads + h)),
                  pl.BlockSpec((None, 2 * NA_ROWS, GRID_W, 2 * GRID_W), lambda b, h, r: (h, 0, 0, 0))],
        out_specs=pl.BlockSpec((rb * GRID_W, hd), lambda b, h, r: (b * nrb + r, h)),
        out_shape=jax.ShapeDtypeStruct((batch * n, n_heads * hd), BF16),
        compiler_params=_params(("arbitrary", "arbitrary", "arbitrary")), name="na_attention",
    )(qkv_l, qkv_l, qkv_l, qkv_c, qkv_c, bias)


def _rope_tables(n):
    nf = HEAD_DIM // 4
    t = jnp.arange(n)
    inv = ROPE_BASE ** (-jnp.arange(nf, dtype=F32) / nf)
    ang_r = (t // GRID_W).astype(F32)[:, None] * inv
    ang_c = (t % GRID_W).astype(F32)[:, None] * inv
    cos = jnp.concatenate([jnp.cos(ang_r), jnp.cos(ang_r), jnp.cos(ang_c), jnp.cos(ang_c)], axis=-1)
    sin = jnp.concatenate([-jnp.sin(ang_r), jnp.sin(ang_r), -jnp.sin(ang_c), jnp.sin(ang_c)], axis=-1)
    return cos, sin


def _rope(x, cos, sin_signed):
    nf = HEAD_DIM // 4
    lane = lax.broadcasted_iota(jnp.int32, x.shape, 1)
    first = (lane % (2 * nf)) < nf
    swapped = jnp.where(first, pltpu.roll(x, HEAD_DIM - nf, 1), pltpu.roll(x, nf, 1))
    return x * cos + swapped * sin_signed


def _sink_column(sink_ref, base, g, rows_per_head):
    return jnp.concatenate([jnp.full((rows_per_head, 1), sink_ref[base + j], F32) for j in range(g)], axis=0)


def _win_kernel(sink_ref, q_ref, k_ref, v_ref, kc_ref, vc_ref, cq_ref, sq_ref, ck_ref, sk_ref, o_ref,
                *, g, n, scale, blocks):
    for u in range(blocks):
        rows = pl.ds(u * WIN_BLOCK, WIN_BLOCK)
        _win_block(sink_ref, q_ref.at[rows], k_ref, v_ref, kc_ref, vc_ref, cq_ref.at[rows], sq_ref.at[rows],
                   ck_ref, sk_ref, o_ref.at[rows], pl.program_id(2) * blocks + u, g=g, n=n, scale=scale)


def _win_block(sink_ref, q_ref, k_ref, v_ref, kc_ref, vc_ref, cq_ref, sq_ref, ck_ref, sk_ref, o_ref, blk,
               *, g, n, scale):
    hd = HEAD_DIM
    kvh = pl.program_id(1)
    span = WIN_BLOCK + 2 * WINDOW
    s0 = pl.multiple_of(jnp.clip(blk * WIN_BLOCK - WINDOW, 0, n - span), LANES)
    cq = cq_ref[...]
    sq = sq_ref[...]
    q = q_ref[...]
    qs = jnp.concatenate([_rope(q[:, j * hd:(j + 1) * hd].astype(F32), cq, sq) for j in range(g)],
                         axis=0).astype(BF16)
    kw = _rope(k_ref[pl.ds(s0, span), :].astype(F32), ck_ref[pl.ds(s0, span), :],
               sk_ref[pl.ds(s0, span), :]).astype(BF16)
    vw = v_ref[pl.ds(s0, span), :]
    s = lax.dot_general(qs, kw, _NT, preferred_element_type=F32) * scale
    qpos = blk * WIN_BLOCK + lax.broadcasted_iota(jnp.int32, (g * WIN_BLOCK, 1), 0) % WIN_BLOCK
    kpos = s0 + lax.broadcasted_iota(jnp.int32, (1, span), 1)
    s = jnp.where(jnp.abs(qpos - kpos) <= WINDOW, s, NEG_INF)
    sc = lax.dot_general(qs, kc_ref[...], _NT, preferred_element_type=F32) * scale
    sink = _sink_column(sink_ref, kvh * g, g, WIN_BLOCK)
    mx = jnp.maximum(jnp.maximum(jnp.max(s, axis=-1, keepdims=True), jnp.max(sc, axis=-1, keepdims=True)), sink)
    p = jnp.exp(s - mx)
    pc = jnp.exp(sc - mx)
    den = jnp.sum(p, axis=-1, keepdims=True) + jnp.sum(pc, axis=-1, keepdims=True) + jnp.exp(sink - mx)
    o = (jnp.dot(p.astype(BF16), vw, preferred_element_type=F32)
         + jnp.dot(pc.astype(BF16), vc_ref[...], preferred_element_type=F32)) / den
    o_ref[...] = jnp.concatenate([o[j * WIN_BLOCK:(j + 1) * WIN_BLOCK] for j in range(g)],
                                 axis=1).astype(o_ref.dtype)


def _win_attention(sink, qkv_l, qkv_c, cos, sin, batch, n, lc, n_na, n_q, n_kv):
    hd = HEAD_DIM
    g = n_q // n_kv
    blocks = _pick(n // WIN_BLOCK, (WIN_STEP_BLOCKS, 1))
    tq = WIN_BLOCK * blocks
    nb = n // tq
    q_blk0 = (3 * n_na * hd) // (g * hd)
    k_blk0 = 3 * n_na + n_q
    assert (3 * n_na) % g == 0
    kern = functools.partial(_win_kernel, g=g, n=n, scale=hd ** -0.5, blocks=blocks)
    full = lambda b, kv, i: (0, 0)
    return pl.pallas_call(
        kern, grid=(batch, n_kv, nb),
        in_specs=[pl.BlockSpec(memory_space=pltpu.SMEM),
                  pl.BlockSpec((tq, g * hd), lambda b, kv, i: (b * nb + i, q_blk0 + kv)),
                  pl.BlockSpec((n, hd), lambda b, kv, i: (b, k_blk0 + kv)),
                  pl.BlockSpec((n, hd), lambda b, kv, i: (b, k_blk0 + n_kv + kv)),
                  pl.BlockSpec((lc, hd), lambda b, kv, i: (b, k_blk0 + kv)),
                  pl.BlockSpec((lc, hd), lambda b, kv, i: (b, k_blk0 + n_kv + kv)),
                  pl.BlockSpec((tq, hd), lambda b, kv, i: (i, 0)),
                  pl.BlockSpec((tq, hd), lambda b, kv, i: (i, 0)),
                  pl.BlockSpec((n, hd), full),
                  pl.BlockSpec((n, hd), full)],
        out_specs=pl.BlockSpec((tq, g * hd), lambda b, kv, i: (b * nb + i, kv)),
        out_shape=jax.ShapeDtypeStruct((batch * n, n_q * hd), BF16),
        compiler_params=_params(("arbitrary", "arbitrary", "arbitrary")), name="window_attention",
    )(sink, qkv_l, qkv_l, qkv_l, qkv_c, qkv_c, cos, sin, cos, sin)


def _ctx_attn_kernel(sink_ref, q_ref, k_ref, v_ref, o_ref, *, g, lc, use_sink, scale):
    hd = HEAD_DIM
    q = q_ref[...]
    qs = jnp.concatenate([q[:, j * hd:(j + 1) * hd] for j in range(g)], axis=0)
    s = lax.dot_general(qs, k_ref[...], _NT, preferred_element_type=F32) * scale
    mx = jnp.max(s, axis=-1, keepdims=True)
    if use_sink:
        sink = _sink_column(sink_ref, pl.program_id(1) * g, g, lc)
        mx = jnp.maximum(mx, sink)
    p = jnp.exp(s - mx)
    den = jnp.sum(p, axis=-1, keepdims=True)
    if use_sink:
        den = den + jnp.exp(sink - mx)
    o = jnp.dot(p.astype(BF16), v_ref[...], preferred_element_type=F32) / den
    o_ref[...] = jnp.concatenate([o[j * lc:(j + 1) * lc] for j in range(g)], axis=1).astype(o_ref.dtype)


def _ctx_attention(sink, qkv_c, batch, lc, n_groups, g, q_blk0, k_blk0, v_blk0, use_sink):
    hd = HEAD_DIM
    kern = functools.partial(_ctx_attn_kernel, g=g, lc=lc, use_sink=use_sink, scale=hd ** -0.5)
    return pl.pallas_call(
        kern, grid=(batch, n_groups),
        in_specs=[pl.BlockSpec(memory_space=pltpu.SMEM),
                  pl.BlockSpec((lc, g * hd), lambda b, h: (b, q_blk0 + h)),
                  pl.BlockSpec((lc, hd), lambda b, h: (b, k_blk0 + h)),
                  pl.BlockSpec((lc, hd), lambda b, h: (b, v_blk0 + h))],
        out_specs=pl.BlockSpec((lc, g * hd), lambda b, h: (b, h)),
        out_shape=jax.ShapeDtypeStruct((batch * lc, n_groups * g * hd), BF16),
        compiler_params=_params(("arbitrary", "arbitrary")), name="ctx_attention",
    )(sink, qkv_c, qkv_c, qkv_c)


def _log_sigmoid(x):
    return jnp.minimum(x, 0.0) - jnp.log(1.0 + jnp.exp(-jnp.abs(x)))


def _mlstm_kernel(*refs, chunk, dv, nchunks, reverse, i_off, f_off, fuse_out, qscale):
    if fuse_out:
        (q_ref, k_ref, v_ref, g_ref, c0_ref, m0_ref, hb_ref, og_ref, nw_ref,
         h_ref, ct_ref, mt_ref, c_scr, m_scr) = refs
    else:
        q_ref, k_ref, v_ref, g_ref, c0_ref, m0_ref, h_ref, ct_ref, mt_ref, c_scr, m_scr = refs
    head = pl.program_id(1)
    step = pl.program_id(2)

    @pl.when(step == 0)
    def _():
        c_scr[...] = c0_ref[...]
        m_scr[...] = m0_ref[...]

    gates = g_ref[...]
    lane = lax.broadcasted_iota(jnp.int32, gates.shape, 1)
    i_col = jnp.sum(jnp.where(lane == i_off + head, gates, 0.0), axis=1, keepdims=True)
    f_col = jnp.sum(jnp.where(lane == f_off + head, gates, 0.0), axis=1, keepdims=True)
    gates_t = gates.T
    sub = lax.broadcasted_iota(jnp.int32, gates_t.shape, 0)
    i_row = jnp.sum(jnp.where(sub == i_off + head, gates_t, 0.0), axis=0, keepdims=True)
    f_row = jnp.sum(jnp.where(sub == f_off + head, gates_t, 0.0), axis=0, keepdims=True)
    lf_col = _log_sigmoid(f_col)
    lf_row = _log_sigmoid(f_row)

    tt = lax.broadcasted_iota(jnp.int32, (chunk, chunk), 0)
    ss = lax.broadcasted_iota(jnp.int32, (chunk, chunk), 1)
    causal = (ss >= tt) if reverse else (ss <= tt)
    anti = (tt >= ss) if reverse else (tt <= ss)
    b_col = jnp.sum(jnp.where(causal, lf_row, 0.0), axis=1, keepdims=True)
    b_row = jnp.sum(jnp.where(anti, lf_col, 0.0), axis=0, keepdims=True)
    g_tot = jnp.sum(lf_row, axis=1, keepdims=True)

    m_old = m_scr[:, 0:1]
    a_row = g_tot - b_row + i_row
    a_col = g_tot - b_col + i_col
    m_new = jnp.maximum(g_tot + m_old, jnp.max(a_row, axis=1, keepdims=True))
    wk_col = jnp.exp(a_col - m_new)
    decay = jnp.exp(g_tot + m_old - m_new)

    q = q_ref[...]
    k = k_ref[...]
    v_aug = jnp.concatenate([v_ref[...], jnp.ones((chunk, LANES), BF16)], axis=1)
    c_old = c_scr[...]

    inter = jnp.dot(q, c_old.astype(BF16), preferred_element_type=F32) * qscale
    qk = lax.dot_general(q, k, _NT, preferred_element_type=F32) * qscale
    dmat = jnp.where(causal, b_col - b_row + i_row, -jnp.inf)
    m_inter = b_col + m_old
    m_t = jnp.maximum(m_inter, jnp.max(dmat, axis=1, keepdims=True))
    w_inter = jnp.exp(m_inter - m_t)
    pw = jnp.exp(dmat - m_t) * qk
    num = w_inter * inter + jnp.dot(pw.astype(BF16), v_aug, preferred_element_type=F32)
    den = num[:, dv:dv + 1]
    hc = num[:, :dv] / jnp.maximum(jnp.abs(den), jnp.exp(-m_t))

    kw = (k.astype(F32) * wk_col).astype(BF16)
    c_scr[...] = decay * c_old + lax.dot_general(kw, v_aug, _TN, preferred_element_type=F32)
    m_scr[...] = jnp.broadcast_to(m_new, m_scr.shape)

    if fuse_out:
        hs = hc + hb_ref[...]
        hn = hs * lax.rsqrt(jnp.mean(hs * hs, axis=-1, keepdims=True) + NORM_EPS) * nw_ref[...]
        h_ref[...] = (hn * jax.nn.sigmoid(og_ref[...].astype(F32))).astype(h_ref.dtype)
    else:
        h_ref[...] = hc

    @pl.when(step == nchunks - 1)
    def _():
        ct_ref[...] = c_scr[...]
        mt_ref[...] = m_scr[...]


def _mlstm_scan(proj, gates, state, n_tok, reverse, fuse=None):
    nh = MLSTM_HEADS
    m = proj.shape[0]
    dv = proj.shape[1] // (3 * nh)
    dqk = dv // 2
    batch = m // n_tok
    chunk = min(MLSTM_KERNEL_CHUNK, n_tok)
    nchunks = n_tok // chunk
    c0, m0 = state
    cidx = (lambda c: nchunks - 1 - c) if reverse else (lambda c: c)
    row = lambda b, c: b * nchunks + cidx(c)
    i_off, f_off = (2 * nh, 3 * nh) if reverse else (0, nh)
    in_specs = [pl.BlockSpec((chunk, dqk), lambda b, h, c: (row(b, c), h)),
                pl.BlockSpec((chunk, dqk), lambda b, h, c: (row(b, c), nh + h)),
                pl.BlockSpec((chunk, dv), lambda b, h, c: (row(b, c), nh + h)),
                pl.BlockSpec((chunk, LANES), lambda b, h, c: (row(b, c), 0)),
                pl.BlockSpec((None, None, dqk, dv + LANES), lambda b, h, c: (b, h, 0, 0)),
                pl.BlockSpec((None, None, 1, LANES), lambda b, h, c: (b, h, 0, 0))]
    args = [proj, proj, proj, gates, c0, m0]
    if fuse is not None:
        h_other, norm_w = fuse
        in_specs += [pl.BlockSpec((chunk, dv), lambda b, h, c: (row(b, c), h)),
                     pl.BlockSpec((chunk, dv), lambda b, h, c: (row(b, c), 2 * nh + h)),
                     pl.BlockSpec((1, dv), lambda b, h, c: (0, h))]
        args += [h_other, proj, norm_w.reshape(1, nh * dv)]
    kern = functools.partial(_mlstm_kernel, chunk=chunk, dv=dv, nchunks=nchunks, reverse=reverse,
                             i_off=i_off, f_off=f_off, fuse_out=fuse is not None, qscale=dqk ** -0.5)
    return pl.pallas_call(
        kern, grid=(batch, nh, nchunks), in_specs=in_specs,
        out_specs=[pl.BlockSpec((chunk, dv), lambda b, h, c: (row(b, c), h)),
                   pl.BlockSpec((None, None, dqk, dv + LANES), lambda b, h, c: (b, h, 0, 0)),
                   pl.BlockSpec((None, None, 1, LANES), lambda b, h, c: (b, h, 0, 0))],
        out_shape=[jax.ShapeDtypeStruct((m, nh * dv), F32 if fuse is None else BF16),
                   jax.ShapeDtypeStruct(c0.shape, F32),
                   jax.ShapeDtypeStruct(m0.shape, F32)],
        scratch_shapes=[pltpu.VMEM((dqk, dv + LANES), F32), pltpu.VMEM((1, LANES), F32)],
        compiler_params=_params(("arbitrary", "arbitrary", "arbitrary")),
        name="mlstm_scan_bwd" if reverse else "mlstm_scan_fwd")(*args)


def _excl_prefix(mask_f, upper):
    n_exp, t = mask_f.shape
    off = jnp.zeros((n_exp, 1), F32)
    parts = []
    for j in range(t // LANES):
        blk = mask_f[:, j * LANES:(j + 1) * LANES]
        parts.append(jnp.dot(blk.astype(BF16), upper, preferred_element_type=F32) + off)
        off = off + jnp.sum(blk, axis=1, keepdims=True)
    return jnp.concatenate(parts, axis=1)


def _select_kernel(aff_ref, idx_ref, gate_ref, slot_scr, *, cap, n_tok):
    e = pl.program_id(1)
    n_exp, t = aff_ref.shape

    @pl.when(e == 0)
    def _():
        a = aff_ref[...]
        bits = pltpu.bitcast(a, jnp.int32)
        thr = jnp.zeros((n_exp, 1), jnp.int32)
        for bit in range(30, -1, -1):
            cand = thr | (1 << bit)
            cnt = jnp.sum(jnp.where(bits >= cand, 1.0, 0.0), axis=1, keepdims=True)
            thr = jnp.where(cnt >= cap, cand, thr)
        uu = lax.broadcasted_iota(jnp.int32, (LANES, LANES), 0)
        vv = lax.broadcasted_iota(jnp.int32, (LANES, LANES), 1)
        upper = jnp.where(uu < vv, 1.0, 0.0).astype(BF16)
        gt = jnp.where(bits > thr, 1.0, 0.0)
        eq = jnp.where(bits == thr, 1.0, 0.0)
        need = cap - jnp.sum(gt, axis=1, keepdims=True)
        sel = gt + eq * jnp.where(_excl_prefix(eq, upper) < need, 1.0, 0.0)
        slot_scr[...] = jnp.where(sel > 0.5, _excl_prefix(sel, upper), -1.0)

    a = aff_ref[pl.ds(e, 1), :]
    slot = slot_scr[pl.ds(e, 1), :]
    a_hi = a.astype(BF16).astype(F32)
    a_mid = (a - a_hi).astype(BF16).astype(F32)
    a_lo = a - a_hi - a_mid
    tpos = lax.broadcasted_iota(jnp.int32, (1, t), 1)
    t_hi = (tpos >> 6).astype(F32)
    t_lo = (tpos & 63).astype(F32)
    row = lax.broadcasted_iota(jnp.int32, (16, t), 0)
    lhs = jnp.where(row == 0, t_hi, jnp.where(row == 1, t_lo, jnp.where(row == 2, a_hi, jnp.where(
        row == 3, a_mid, jnp.where(row == 4, a_lo, 0.0))))).astype(BF16)
    s_iota = lax.broadcasted_iota(jnp.int32, (cap, t), 0).astype(F32)
    onehot = jnp.where(slot == s_iota, 1.0, 0.0).astype(BF16)
    r = lax.dot_general(lhs, onehot, _NT, preferred_element_type=F32)
    idx_ref[...] = (r[0:1] * 64.0 + r[1:2]).astype(jnp.int32) + pl.program_id(0) * n_tok
    gate_ref[...] = r[2:3] + r[3:4] + r[4:5]


def _select(aff, cap):
    batch, n_exp, n_tok = aff.shape
    kern = functools.partial(_select_kernel, cap=cap, n_tok=n_tok)
    out_spec = pl.BlockSpec((None, None, 1, cap), lambda b, e: (b, e, 0, 0))
    return pl.pallas_call(
        kern, grid=(batch, n_exp),
        in_specs=[pl.BlockSpec((None, n_exp, n_tok), lambda b, e: (b, 0, 0))],
        out_specs=[out_spec, out_spec],
        out_shape=[jax.ShapeDtypeStruct((batch, n_exp, 1, cap), jnp.int32),
                   jax.ShapeDtypeStruct((batch, n_exp, 1, cap), F32)],
        scratch_shapes=[pltpu.VMEM((n_exp, n_tok), F32)],
        compiler_params=_params(("arbitrary", "arbitrary")), name="moe_select")(aff)


def _moe_ffn_kernel(idx_ref, h_hbm, x_hbm, w1_ref, w3_ref, w2_ref, gate_ref, g2_ref, o_hbm,
                    h_buf, x_buf, sem_h, sem_x, sem_o, *, tc, n_steps):
    t = (pl.program_id(0) * pl.num_programs(1) + pl.program_id(1)) * pl.num_programs(2) + pl.program_id(2)

    def h_copy(step, s, rows=None):
        row = idx_ref[(step if rows is None else rows) * tc + s]
        return pltpu.make_async_copy(h_hbm.at[pl.ds(row, 1)], h_buf.at[step % 2, pl.ds(s, 1)], sem_h.at[step % 2])

    def x_copy(step, s, rows=None):
        row = idx_ref[(step if rows is None else rows) * tc + s]
        return pltpu.make_async_copy(x_hbm.at[pl.ds(row, 1)], x_buf.at[step % 3, pl.ds(s, 1)], sem_x.at[step % 3])

    def o_copy(step, s, rows=None):
        row = idx_ref[(step if rows is None else rows) * tc + s]
        return pltpu.make_async_copy(x_buf.at[step % 3, pl.ds(s, 1)], o_hbm.at[pl.ds(row, 1)], sem_o.at[step % 3])

    def wait_h(step):
        pltpu.make_async_copy(h_hbm.at[pl.ds(0, tc)], h_buf.at[step % 2], sem_h.at[step % 2]).wait()

    def wait_x(step):
        pltpu.make_async_copy(x_hbm.at[pl.ds(0, tc)], x_buf.at[step % 3], sem_x.at[step % 3]).wait()

    def wait_o(step):
        pltpu.make_async_copy(x_buf.at[step % 3], o_hbm.at[pl.ds(0, tc)], sem_o.at[step % 3]).wait()

    @pl.when(t == 0)
    def _():
        def body(s, carry):
            h_copy(0, s).start()
            x_copy(0, s).start()
            return carry
        lax.fori_loop(0, tc, body, 0, unroll=DMA_UNROLL)

    @pl.when(t >= 2)
    def _():
        wait_o(t - 2)
    wait_h(t)
    wait_x(t)

    nxt = (t + 1) % n_steps
    for s in range(tc):
        h_copy(t + 1, s, nxt).start()
        x_copy(t + 1, s, nxt).start()

    x = h_buf[t % 2].astype(BF16)
    a = jnp.dot(x, w1_ref[...], preferred_element_type=F32)
    u = jnp.dot(x, w3_ref[...], preferred_element_type=F32)
    hmid = (a * jax.nn.sigmoid(a) * u).astype(BF16)
    y = jnp.dot(hmid, w2_ref[...], preferred_element_type=F32)
    ii = lax.broadcasted_iota(jnp.int32, (tc, tc), 0)
    jj = lax.broadcasted_iota(jnp.int32, (tc, tc), 1)
    gate_col = jnp.sum(jnp.where(ii == jj, gate_ref[...], 0.0), axis=1, keepdims=True)
    x_buf[t % 3] = x_buf[t % 3] + g2_ref[...] * (y * gate_col)
    for s in range(tc):
        o_copy(t, s).start()

    @pl.when(t == n_steps - 1)
    def _():
        if n_steps >= 2:
            wait_o(t - 1)
        wait_o(t)
        wait_h(t + 1)
        wait_x(t + 1)


def _moe_ffn(idx, gate, h2, x, w1, w3, w2, gate2):
    batch, n_exp, _, cap = idx.shape
    m, d = x.shape
    ff = w1.shape[-1]
    tc = min(MOE_ROW_TILE, cap)
    tiles = cap // tc
    assert cap % tc == 0 and batch * tiles - (tiles - 1) >= 3
    n_steps = n_exp * batch * tiles
    g_idx = (lambda e, b, j, i: (b, 0, 0)) if gate2.shape[0] > 1 else (lambda e, b, j, i: (0, 0, 0))
    kern = functools.partial(_moe_ffn_kernel, tc=tc, n_steps=n_steps)
    grid_spec = pltpu.PrefetchScalarGridSpec(
        num_scalar_prefetch=1, grid=(n_exp, batch, tiles),
        in_specs=[pl.BlockSpec(memory_space=pl.ANY),
                  pl.BlockSpec(memory_space=pl.ANY),
                  pl.BlockSpec((None, d, ff), lambda e, b, j, i: (e, 0, 0)),
                  pl.BlockSpec((None, d, ff), lambda e, b, j, i: (e, 0, 0)),
                  pl.BlockSpec((None, ff, d), lambda e, b, j, i: (e, 0, 0)),
                  pl.BlockSpec((None, None, 1, tc), lambda e, b, j, i: (b, e, 0, j)),
                  pl.BlockSpec((None, 1, d), g_idx)],
        out_specs=pl.BlockSpec(memory_space=pl.ANY),
        scratch_shapes=[pltpu.VMEM((2, tc, d), F32), pltpu.VMEM((3, tc, d), F32),
                        pltpu.SemaphoreType.DMA((2,)), pltpu.SemaphoreType.DMA((3,)), pltpu.SemaphoreType.DMA((3,))])
    idx_steps = jnp.transpose(idx.reshape(batch, n_exp, cap), (1, 0, 2)).reshape(-1)
    return pl.pallas_call(
        kern, grid_spec=grid_spec,
        out_shape=jax.ShapeDtypeStruct((m, d), F32),
        input_output_aliases={2: 0},
        compiler_params=_params(("arbitrary", "arbitrary", "arbitrary")), name="moe_ffn",
    )(idx_steps, h2, x, w1, w3, w2, gate, gate2)


def _moe(x, norm_w, shift, scale, gate2, router_t, w1, w3, w2, n_tok):
    n_exp = router_t.shape[0]
    cap = (CAPACITY_FACTOR * n_tok) // n_exp
    h2, aff = _norm_mod(x, norm_w, shift, scale, n_tok, router_t=router_t)
    idx, gate = _select(aff, cap)
    return _moe_ffn(idx, gate, h2, x, w1, w3, w2, gate2)


def kernel(x, c, ctx, c_ctx, ada_w, ada_b, norm1_w, norm2_w, ab_w_in, ab_w_out, na_rpb, win_sink,
           ml_w_in, ml_b_gates, ml_norm_w, ml_w_out, moe_router, moe_w1, moe_w3, moe_w2, final_norm_w):
    batch, n, d = x.shape
    lc = ctx.shape[1]
    depth = ada_w.shape[0]
    hd = HEAD_DIM
    n_heads = d // hd
    n_na = n_heads // 2
    n_q = n_heads - n_na
    n_kv = max(1, n_q // 4)
    nh = MLSTM_HEADS
    dv = d // nh
    dqk = dv // 2
    assert batch + 1 <= 8

    xl = x.reshape(batch * n, d)
    xc = ctx.reshape(batch * lc, d)
    cc = jnp.concatenate([c, c_ctx[None], jnp.zeros((8 - batch - 1, d), F32)], axis=0)
    mods = _ada_mods(cc, ada_w, ada_b)
    cos, sin = _rope_tables(n)

    for l in range(depth):
        need_ctx = l < depth - 1
        ml = mods[l].reshape(8, ADA_MODS, 1, d)
        sh1, sc1, g1, sh2, sc2, g2 = (ml[:batch, j] for j in range(ADA_MODS))
        ch1, cs1, cg1, ch2, cs2, cg2 = (ml[batch:batch + 1, j] for j in range(ADA_MODS))
        hl = _norm_mod(xl, norm1_w[l], sh1, sc1, n)
        hc = _norm_mod(xc, norm1_w[l], ch1, cs1, lc)
        if l % 2 == 0:
            e = l // 2
            qkv_l = _proj(hl, ab_w_in, e, ab_w_in.shape[2])
            qkv_c = _proj(hc, ab_w_in, e, ab_w_in.shape[2])
            bias = _na_bias_pairs(na_rpb[e])
            na_l = _na_attention(qkv_l, qkv_c, bias, batch, n, lc, n_na)
            win_l = _win_attention(win_sink[e], qkv_l, qkv_c, cos, sin, batch, n, lc, n_na, n_q, n_kv)
            xl = _proj_residual([na_l, win_l], ab_w_out, e, xl, g1, n, in_place=l > 0)
            if need_ctx:
                g = n_q // n_kv
                na_c = _ctx_attention(win_sink[e], qkv_c, batch, lc, n_na, 1, 0, n_na, 2 * n_na, False)
                win_c = _ctx_attention(win_sink[e], qkv_c, batch, lc, n_kv, g,
                                       (3 * n_na) // g, 3 * n_na + n_q, 3 * n_na + n_q + n_kv, True)
                xc = _proj_residual([na_c, win_c], ab_w_out, e, xc, cg1, lc, in_place=l > 0)
        else:
            o = l // 2
            main_w = 2 * nh * dqk + 2 * nh * dv
            n_gates = ml_w_in.shape[2] - main_w
            w_gate = jnp.pad(ml_w_in[o, :, main_w:], ((0, 0), (0, LANES - n_gates))).astype(BF16)
            b_gate = jnp.pad(ml_b_gates[o], (0, LANES - n_gates))
            pl_l, pl_c = _proj(hl, ml_w_in, o, main_w), _proj(hc, ml_w_in, o, main_w)
            gt_l = _gate_proj(hl, w_gate, b_gate)
            gt_c = _gate_proj(hc, w_gate, b_gate)
            state0 = (jnp.zeros((batch, nh, dqk, dv + LANES), F32), jnp.zeros((batch, nh, 1, LANES), F32))
            hcb, cb, mb = _mlstm_scan(pl_c, gt_c, state0, lc, True)
            hlb, _, _ = _mlstm_scan(pl_l, gt_l, (cb, mb), n, True)
            yc, cf, mf = _mlstm_scan(pl_c, gt_c, state0, lc, False, fuse=(hcb, ml_norm_w[o]))
            yl, _, _ = _mlstm_scan(pl_l, gt_l, (cf, mf), n, False, fuse=(hlb, ml_norm_w[o]))
            xl = _proj_residual([yl], ml_w_out, o, xl, g1, n, in_place=l > 0)
            if need_ctx:
                xc = _proj_residual([yc], ml_w_out, o, xc, cg1, lc, in_place=l > 0)
        router_t = moe_router[l].T.astype(BF16)
        w1, w3, w2 = moe_w1[l].astype(BF16), moe_w3[l].astype(BF16), moe_w2[l].astype(BF16)
        xl = _moe(xl, norm2_w[l], sh2, sc2, g2, router_t, w1, w3, w2, n)
        if need_ctx:
            xc = _moe(xc, norm2_w[l], ch2, cs2, cg2, router_t, w1, w3, w2, lc)

    zeros = jnp.zeros((1, 1, d), F32)
    out = _norm_mod(xl, final_norm_w, zeros, zeros, n, out_dtype=F32)
    return out.reshape(batch, n, d)
```

```python
import functools

import numpy as np
import jax
import jax.numpy as jnp
from jax import lax
from jax.experimental import pallas as pl
from jax.experimental.pallas import tpu as pltpu

GRID_W = 64
HEAD_DIM = 128
NA_ROWS = 8
NA_COLS = 16
NA_BLOCK_ROWS = 4
WINDOW = 128
WIN_BLOCK = 128
WIN_STEP_BLOCKS = 2
ROPE_BASE = 10000.0
MLSTM_HEADS = 8
MLSTM_KERNEL_CHUNK = 256
CAPACITY_FACTOR = 2
ADA_MODS = 6
NORM_EPS = 1e-6
NEG_INF = -1e30
LANES = 128
MOE_ROW_TILE = 256
DMA_UNROLL = 8
V7X_VMEM_LIMIT_BYTES = 56 * 1024 * 1024

F32 = jnp.float32
BF16 = jnp.bfloat16
_NT = (((1,), (1,)), ((), ()))
_TN = (((0,), (0,)), ((), ()))


def _pick(n, cands):
    for c in cands:
        if n % c == 0:
            return c
    raise ValueError(f"no tile in {cands} divides {n}")


def _params(sem):
    return pltpu.CompilerParams(dimension_semantics=sem, vmem_limit_bytes=V7X_VMEM_LIMIT_BYTES)


def _ada_kernel(cc_ref, w_ref, b_ref, o_ref):
    x = cc_ref[...]
    s = x * jax.nn.sigmoid(x)
    o_ref[...] = jnp.dot(s.astype(BF16), w_ref[...].astype(BF16), preferred_element_type=F32) + b_ref[...]


def _ada_mods(cc, ada_w, ada_b):
    depth, d, n = ada_w.shape
    tn = _pick(n, (512, 256, 128))
    return pl.pallas_call(
        _ada_kernel,
        grid=(depth, n // tn),
        in_specs=[pl.BlockSpec((8, d), lambda l, j: (0, 0)),
                  pl.BlockSpec((None, d, tn), lambda l, j: (l, 0, j)),
                  pl.BlockSpec((None, 1, tn), lambda l, j: (l, 0, j))],
        out_specs=pl.BlockSpec((None, 8, tn), lambda l, j: (l, 0, j)),
        out_shape=jax.ShapeDtypeStruct((depth, 8, n), F32),
        compiler_params=_params(("arbitrary", "arbitrary")),
        name="ada_mods",
    )(cc, ada_w, ada_b.reshape(depth, 1, n))


def _norm_body(x_ref, w_ref, sh_ref, sc_ref):
    x = x_ref[...]
    y = x * lax.rsqrt(jnp.mean(x * x, axis=-1, keepdims=True) + NORM_EPS) * w_ref[...]
    return y * (1.0 + sc_ref[...]) + sh_ref[...]


def _norm_kernel(x_ref, w_ref, sh_ref, sc_ref, o_ref):
    o_ref[...] = _norm_body(x_ref, w_ref, sh_ref, sc_ref).astype(o_ref.dtype)


def _norm_router_kernel(x_ref, w_ref, sh_ref, sc_ref, wr_ref, o_ref, aff_ref):
    h = _norm_body(x_ref, w_ref, sh_ref, sc_ref)
    o_ref[...] = h
    logits = lax.dot_general(wr_ref[...], h.astype(BF16), _NT, preferred_element_type=F32)
    e = jnp.exp(logits - jnp.max(logits, axis=0, keepdims=True))
    aff_ref[...] = e / jnp.sum(e, axis=0, keepdims=True)


def _norm_mod(x, w, shift, scale, n_tok, out_dtype=BF16, router_t=None):
    m, d = x.shape
    tm = _pick(n_tok, (512, 256, 128))
    tiles = n_tok // tm
    per_sample = shift.shape[0] > 1
    mod_idx = (lambda i: (i // tiles, 0, 0)) if per_sample else (lambda i: (0, 0, 0))
    in_specs = [pl.BlockSpec((tm, d), lambda i: (i, 0)),
                pl.BlockSpec((1, d), lambda i: (0, 0)),
                pl.BlockSpec((None, 1, d), mod_idx),
                pl.BlockSpec((None, 1, d), mod_idx)]
    args = [x, w.reshape(1, d), shift, scale]
    o_spec = pl.BlockSpec((tm, d), lambda i: (i, 0))
    if router_t is None:
        return pl.pallas_call(
            _norm_kernel, grid=(m // tm,), in_specs=in_specs, out_specs=o_spec,
            out_shape=jax.ShapeDtypeStruct((m, d), out_dtype),
            compiler_params=_params(("arbitrary",)), name="norm_mod")(*args)
    n_exp = router_t.shape[0]
    return pl.pallas_call(
        _norm_router_kernel, grid=(m // tm,),
        in_specs=in_specs + [pl.BlockSpec((n_exp, d), lambda i: (0, 0))],
        out_specs=[o_spec, pl.BlockSpec((None, n_exp, tm), lambda i: (i // tiles, 0, i % tiles))],
        out_shape=[jax.ShapeDtypeStruct((m, d), F32),
                   jax.ShapeDtypeStruct((m // n_tok, n_exp, n_tok), F32)],
        compiler_params=_params(("arbitrary",)), name="norm_mod_router")(*args, router_t)


def _cast_weight_once(w_ref, wb_scr):
    @pl.when(pl.program_id(1) == 0)
    def _():
        wb_scr[...] = w_ref[...].astype(BF16)


def _proj_kernel(a_ref, w_ref, o_ref, wb_scr):
    _cast_weight_once(w_ref, wb_scr)
    o_ref[...] = jnp.dot(a_ref[...], wb_scr[...], preferred_element_type=F32).astype(o_ref.dtype)


def _proj_res_kernel(*refs):
    *a_refs, w_ref, r_ref, g_ref, o_ref, wb_scr = refs
    _cast_weight_once(w_ref, wb_scr)
    acc, k0 = None, 0
    for a_ref in a_refs:
        k1 = k0 + a_ref.shape[1]
        part = jnp.dot(a_ref[...], wb_scr[k0:k1, :], preferred_element_type=F32)
        acc, k0 = part if acc is None else acc + part, k1
    o_ref[...] = r_ref[...] + g_ref[...] * acc


def _mm_bias_kernel(a_ref, w_ref, b_ref, o_ref):
    o_ref[...] = jnp.dot(a_ref[...], w_ref[...], preferred_element_type=F32) + b_ref[...]


def _proj(a, w_stack, layer, n_cols, out_dtype=BF16):
    m, k = a.shape
    tm = _pick(m, (1024, 512, 256, 128))
    tn = _pick(n_cols, (512, 256, 128))
    return pl.pallas_call(
        _proj_kernel, grid=(n_cols // tn, m // tm),
        in_specs=[pl.BlockSpec((tm, k), lambda j, i: (i, 0)),
                  pl.BlockSpec((None, k, tn), lambda j, i: (layer, 0, j))],
        out_specs=pl.BlockSpec((tm, tn), lambda j, i: (i, j)),
        out_shape=jax.ShapeDtypeStruct((m, n_cols), out_dtype),
        scratch_shapes=[pltpu.VMEM((k, tn), BF16)],
        compiler_params=_params(("arbitrary", "arbitrary")), name="proj")(a, w_stack)


def _proj_residual(a_parts, w_stack, layer, res, gate, n_tok, in_place=True):
    m = a_parts[0].shape[0]
    _, k, n = w_stack.shape
    assert sum(a.shape[1] for a in a_parts) == k
    tm = _pick(n_tok, (1024, 512, 256, 128))
    tn = _pick(n, (512, 256, 128))
    tiles = n_tok // tm
    g_idx = (lambda j, i: (i // tiles, 0, j)) if gate.shape[0] > 1 else (lambda j, i: (0, 0, j))
    return pl.pallas_call(
        _proj_res_kernel, grid=(n // tn, m // tm),
        in_specs=[pl.BlockSpec((tm, a.shape[1]), lambda j, i: (i, 0)) for a in a_parts]
                 + [pl.BlockSpec((None, k, tn), lambda j, i: (layer, 0, j)),
                    pl.BlockSpec((tm, tn), lambda j, i: (i, j)),
                    pl.BlockSpec((None, 1, tn), g_idx)],
        out_specs=pl.BlockSpec((tm, tn), lambda j, i: (i, j)),
        out_shape=jax.ShapeDtypeStruct((m, n), F32),
        scratch_shapes=[pltpu.VMEM((k, tn), BF16)],
        input_output_aliases={len(a_parts) + 1: 0} if in_place else {},
        compiler_params=_params(("arbitrary", "arbitrary")), name="proj_residual")(*a_parts, w_stack, res, gate)


def _gate_proj(a, w, bias):
    m, k = a.shape
    n = w.shape[1]
    tm = _pick(m, (1024, 512, 256, 128))
    return pl.pallas_call(
        _mm_bias_kernel, grid=(m // tm,),
        in_specs=[pl.BlockSpec((tm, k), lambda i: (i, 0)), pl.BlockSpec((k, n), lambda i: (0, 0)),
                  pl.BlockSpec((1, n), lambda i: (0, 0))],
        out_specs=pl.BlockSpec((tm, n), lambda i: (i, 0)),
        out_shape=jax.ShapeDtypeStruct((m, n), F32),
        compiler_params=_params(("arbitrary",)), name="gate_proj")(a, w, bias.reshape(1, n))


def _na_bias_pairs(rpb):
    cols = np.arange(GRID_W)
    cstart = np.clip(cols - NA_COLS // 2, 0, GRID_W - NA_COLS)
    valid = (cols[None, :] >= cstart[:, None]) & (cols[None, :] < cstart[:, None] + NA_COLS)
    dc = cols[None, :] - cols[:, None] + NA_COLS - 1
    shift = (np.arange(2 * NA_COLS - 1)[:, None, None] == dc[None]) & valid[None]
    t = jnp.einsum("hrd,dqk->hrqk", rpb.astype(F32), jnp.asarray(shift, F32), precision=lax.Precision.HIGHEST)
    t = jnp.where(valid[None, None], t, NEG_INF)
    t = jnp.pad(t, ((0, 0), (1, 1), (0, 0), (0, 0)), constant_values=NEG_INF)
    return jnp.concatenate([t[:, :-1], t[:, 1:]], axis=-1)


def _na_kernel(q_ref, k_ref, v_ref, kc_ref, vc_ref, bias_ref, o_ref, *, rb, rows, kh, scale):
    gw, br = GRID_W, NA_BLOCK_ROWS
    wr = kh + br
    r0 = pl.program_id(2) * rb
    sc = lax.dot_general(q_ref[...], kc_ref[...], _NT, preferred_element_type=F32) * scale
    mc = jnp.max(sc, axis=-1, keepdims=True)
    mx_blocks, den_blocks, o_blocks = [], [], []
    for bi in range(rb // br):
        r4 = r0 + bi * br
        start = jnp.clip(r4 - kh // 2, 0, rows - wr)
        q = q_ref[bi * br * gw:(bi + 1) * br * gw, :]
        win = pl.ds(pl.multiple_of(start * gw, gw), wr * gw)
        bias = jnp.concatenate(
            [jnp.concatenate([bias_ref[jnp.clip(start - (r4 + a) + NA_ROWS + 2 * p, 0, 2 * NA_ROWS - 1)]
                              for p in range(wr // 2)], axis=1) for a in range(br)], axis=0)
        q_row = r4 + lax.broadcasted_iota(jnp.int32, (br * gw, 1), 0) // gw
        k_row = start + lax.broadcasted_iota(jnp.int32, (1, wr * gw), 1) // gw
        off = k_row - jnp.clip(q_row - kh // 2, 0, rows - kh)
        s = lax.dot_general(q, k_ref[win, :], _NT, preferred_element_type=F32) * scale + bias
        s = jnp.where(off >= 0, jnp.where(off < kh, s, NEG_INF), NEG_INF)
        mx = jnp.maximum(jnp.max(s, axis=-1, keepdims=True), mc[bi * br * gw:(bi + 1) * br * gw])
        p = jnp.exp(s - mx)
        mx_blocks.append(mx)
        den_blocks.append(jnp.sum(p, axis=-1, keepdims=True))
        o_blocks.append(jnp.dot(p.astype(BF16), v_ref[win, :], preferred_element_type=F32))
    pc = jnp.exp(sc - jnp.concatenate(mx_blocks, axis=0))
    den = jnp.concatenate(den_blocks, axis=0) + jnp.sum(pc, axis=-1, keepdims=True)
    o = jnp.concatenate(o_blocks, axis=0) + jnp.dot(pc.astype(BF16), vc_ref[...], preferred_element_type=F32)
    o_ref[...] = (o / den).astype(o_ref.dtype)


def _na_attention(qkv_l, qkv_c, bias, batch, n, lc, n_heads):
    rows = n // GRID_W
    kh = min(NA_ROWS, rows)
    assert kh % 2 == 0 and NA_BLOCK_ROWS % 2 == 0 and rows >= kh + NA_BLOCK_ROWS
    rb = _pick(rows, (2 * NA_BLOCK_ROWS, NA_BLOCK_ROWS))
    nrb = rows // rb
    hd = HEAD_DIM
    kern = functools.partial(_na_kernel, rb=rb, rows=rows, kh=kh, scale=hd ** -0.5)
    return pl.pallas_call(
        kern, grid=(batch, n_heads, nrb),
        in_specs=[pl.BlockSpec((rb * GRID_W, hd), lambda b, h, r: (b * nrb + r, h)),
                  pl.BlockSpec((n, hd), lambda b, h, r: (b, n_heads + h)),
                  pl.BlockSpec((n, hd), lambda b, h, r: (b, 2 * n_heads + h)),
                  pl.BlockSpec((lc, hd), lambda b, h, r: (b, n_heads + h)),
                  pl.BlockSpec((lc, hd), lambda b, h, r: (b, 2 * n_heads + h)),
                  pl.BlockSpec((None, 2 * NA_ROWS, GRID_W, 2 * GRID_W), lambda b, h, r: (h, 0, 0, 0))],
        out_specs=pl.BlockSpec((rb * GRID_W, hd), lambda b, h, r: (b * nrb + r, h)),
        out_shape=jax.ShapeDtypeStruct((batch * n, n_heads * hd), BF16),
        compiler_params=_params(("arbitrary", "arbitrary", "arbitrary")), name="na_attention",
    )(qkv_l, qkv_l, qkv_l, qkv_c, qkv_c, bias)


def _rope_tables(n):
    nf = HEAD_DIM // 4
    t = jnp.arange(n)
    inv = ROPE_BASE ** (-jnp.arange(nf, dtype=F32) / nf)
    ang_r = (t // GRID_W).astype(F32)[:, None] * inv
    ang_c = (t % GRID_W).astype(F32)[:, None] * inv
    cos = jnp.concatenate([jnp.cos(ang_r), jnp.cos(ang_r), jnp.cos(ang_c), jnp.cos(ang_c)], axis=-1)
    sin = jnp.concatenate([-jnp.sin(ang_r), jnp.sin(ang_r), -jnp.sin(ang_c), jnp.sin(ang_c)], axis=-1)
    return cos, sin


def _rope(x, cos, sin_signed):
    nf = HEAD_DIM // 4
    lane = lax.broadcasted_iota(jnp.int32, x.shape, 1)
    first = (lane % (2 * nf)) < nf
    swapped = jnp.where(first, pltpu.roll(x, HEAD_DIM - nf, 1), pltpu.roll(x, nf, 1))
    return x * cos + swapped * sin_signed


def _sink_column(sink_ref, base, g, rows_per_head):
    return jnp.concatenate([jnp.full((rows_per_head, 1), sink_ref[base + j], F32) for j in range(g)], axis=0)


def _win_kernel(sink_ref, q_ref, k_ref, v_ref, kc_ref, vc_ref, cq_ref, sq_ref, ck_ref, sk_ref, o_ref, kr_scr,
                *, g, n, scale, blocks):
    @pl.when(pl.program_id(2) == 0)
    def _():
        kr_scr[...] = _rope(k_ref[...].astype(F32), ck_ref[...], sk_ref[...]).astype(BF16)

    for u in range(blocks):
        rows = pl.ds(u * WIN_BLOCK, WIN_BLOCK)
        _win_block(sink_ref, q_ref.at[rows], kr_scr, v_ref, kc_ref, vc_ref, cq_ref.at[rows], sq_ref.at[rows],
                   o_ref.at[rows], pl.program_id(2) * blocks + u, g=g, n=n, scale=scale)


def _win_block(sink_ref, q_ref, kr_ref, v_ref, kc_ref, vc_ref, cq_ref, sq_ref, o_ref, blk, *, g, n, scale):
    hd = HEAD_DIM
    kvh = pl.program_id(1)
    span = WIN_BLOCK + 2 * WINDOW
    s0 = pl.multiple_of(jnp.clip(blk * WIN_BLOCK - WINDOW, 0, n - span), LANES)
    cq = cq_ref[...]
    sq = sq_ref[...]
    q = q_ref[...]
    qs = jnp.concatenate([_rope(q[:, j * hd:(j + 1) * hd].astype(F32), cq, sq) for j in range(g)],
                         axis=0).astype(BF16)
    kw = kr_ref[pl.ds(s0, span), :]
    vw = v_ref[pl.ds(s0, span), :]
    s = lax.dot_general(qs, kw, _NT, preferred_element_type=F32) * scale
    qpos = blk * WIN_BLOCK + lax.broadcasted_iota(jnp.int32, (g * WIN_BLOCK, 1), 0) % WIN_BLOCK
    kpos = s0 + lax.broadcasted_iota(jnp.int32, (1, span), 1)
    s = jnp.where(jnp.abs(qpos - kpos) <= WINDOW, s, NEG_INF)
    sc = lax.dot_general(qs, kc_ref[...], _NT, preferred_element_type=F32) * scale
    sink = _sink_column(sink_ref, kvh * g, g, WIN_BLOCK)
    mx = jnp.maximum(jnp.maximum(jnp.max(s, axis=-1, keepdims=True), jnp.max(sc, axis=-1, keepdims=True)), sink)
    p = jnp.exp(s - mx)
    pc = jnp.exp(sc - mx)
    den = jnp.sum(p, axis=-1, keepdims=True) + jnp.sum(pc, axis=-1, keepdims=True) + jnp.exp(sink - mx)
    o = (jnp.dot(p.astype(BF16), vw, preferred_element_type=F32)
         + jnp.dot(pc.astype(BF16), vc_ref[...], preferred_element_type=F32)) / den
    o_ref[...] = jnp.concatenate([o[j * WIN_BLOCK:(j + 1) * WIN_BLOCK] for j in range(g)],
                                 axis=1).astype(o_ref.dtype)


def _win_attention(sink, qkv_l, qkv_c, cos, sin, batch, n, lc, n_na, n_q, n_kv):
    hd = HEAD_DIM
    g = n_q // n_kv
    blocks = _pick(n // WIN_BLOCK, (WIN_STEP_BLOCKS, 1))
    tq = WIN_BLOCK * blocks
    nb = n // tq
    q_blk0 = (3 * n_na * hd) // (g * hd)
    k_blk0 = 3 * n_na + n_q
    assert (3 * n_na) % g == 0
    kern = functools.partial(_win_kernel, g=g, n=n, scale=hd ** -0.5, blocks=blocks)
    full = lambda b, kv, i: (0, 0)
    return pl.pallas_call(
        kern, grid=(batch, n_kv, nb),
        in_specs=[pl.BlockSpec(memory_space=pltpu.SMEM),
                  pl.BlockSpec((tq, g * hd), lambda b, kv, i: (b * nb + i, q_blk0 + kv)),
                  pl.BlockSpec((n, hd), lambda b, kv, i: (b, k_blk0 + kv)),
                  pl.BlockSpec((n, hd), lambda b, kv, i: (b, k_blk0 + n_kv + kv)),
                  pl.BlockSpec((lc, hd), lambda b, kv, i: (b, k_blk0 + kv)),
                  pl.BlockSpec((lc, hd), lambda b, kv, i: (b, k_blk0 + n_kv + kv)),
                  pl.BlockSpec((tq, hd), lambda b, kv, i: (i, 0)),
                  pl.BlockSpec((tq, hd), lambda b, kv, i: (i, 0)),
                  pl.BlockSpec((n, hd), full),
                  pl.BlockSpec((n, hd), full)],
        out_specs=pl.BlockSpec((tq, g * hd), lambda b, kv, i: (b * nb + i, kv)),
        out_shape=jax.ShapeDtypeStruct((batch * n, n_q * hd), BF16),
        scratch_shapes=[pltpu.VMEM((n, hd), BF16)],
        compiler_params=_params(("arbitrary", "arbitrary", "arbitrary")), name="window_attention",
    )(sink, qkv_l, qkv_l, qkv_l, qkv_c, qkv_c, cos, sin, cos, sin)


def _ctx_attn_kernel(sink_ref, q_ref, k_ref, v_ref, o_ref, *, g, lc, use_sink, scale):
    hd = HEAD_DIM
    q = q_ref[...]
    qs = jnp.concatenate([q[:, j * hd:(j + 1) * hd] for j in range(g)], axis=0)
    s = lax.dot_general(qs, k_ref[...], _NT, preferred_element_type=F32) * scale
    mx = jnp.max(s, axis=-1, keepdims=True)
    if use_sink:
        sink = _sink_column(sink_ref, pl.program_id(1) * g, g, lc)
        mx = jnp.maximum(mx, sink)
    p = jnp.exp(s - mx)
    den = jnp.sum(p, axis=-1, keepdims=True)
    if use_sink:
        den = den + jnp.exp(sink - mx)
    o = jnp.dot(p.astype(BF16), v_ref[...], preferred_element_type=F32) / den
    o_ref[...] = jnp.concatenate([o[j * lc:(j + 1) * lc] for j in range(g)], axis=1).astype(o_ref.dtype)


def _ctx_attention(sink, qkv_c, batch, lc, n_groups, g, q_blk0, k_blk0, v_blk0, use_sink):
    hd = HEAD_DIM
    kern = functools.partial(_ctx_attn_kernel, g=g, lc=lc, use_sink=use_sink, scale=hd ** -0.5)
    return pl.pallas_call(
        kern, grid=(batch, n_groups),
        in_specs=[pl.BlockSpec(memory_space=pltpu.SMEM),
                  pl.BlockSpec((lc, g * hd), lambda b, h: (b, q_blk0 + h)),
                  pl.BlockSpec((lc, hd), lambda b, h: (b, k_blk0 + h)),
                  pl.BlockSpec((lc, hd), lambda b, h: (b, v_blk0 + h))],
        out_specs=pl.BlockSpec((lc, g * hd), lambda b, h: (b, h)),
        out_shape=jax.ShapeDtypeStruct((batch * lc, n_groups * g * hd), BF16),
        compiler_params=_params(("arbitrary", "arbitrary")), name="ctx_attention",
    )(sink, qkv_c, qkv_c, qkv_c)


def _log_sigmoid(x):
    return jnp.minimum(x, 0.0) - jnp.log(1.0 + jnp.exp(-jnp.abs(x)))


def _mlstm_kernel(*refs, chunk, dv, nchunks, reverse, i_off, f_off, fuse_out, qscale):
    if fuse_out:
        (q_ref, k_ref, v_ref, g_ref, c0_ref, m0_ref, hb_ref, og_ref, nw_ref,
         h_ref, ct_ref, mt_ref, c_scr, m_scr) = refs
    else:
        q_ref, k_ref, v_ref, g_ref, c0_ref, m0_ref, h_ref, ct_ref, mt_ref, c_scr, m_scr = refs
    head = pl.program_id(1)
    step = pl.program_id(2)

    @pl.when(step == 0)
    def _():
        c_scr[...] = c0_ref[...]
        m_scr[...] = m0_ref[...]

    gates = g_ref[...]
    lane = lax.broadcasted_iota(jnp.int32, gates.shape, 1)
    i_col = jnp.sum(jnp.where(lane == i_off + head, gates, 0.0), axis=1, keepdims=True)
    f_col = jnp.sum(jnp.where(lane == f_off + head, gates, 0.0), axis=1, keepdims=True)
    gates_t = gates.T
    sub = lax.broadcasted_iota(jnp.int32, gates_t.shape, 0)
    i_row = jnp.sum(jnp.where(sub == i_off + head, gates_t, 0.0), axis=0, keepdims=True)
    f_row = jnp.sum(jnp.where(sub == f_off + head, gates_t, 0.0), axis=0, keepdims=True)
    lf_col = _log_sigmoid(f_col)
    lf_row = _log_sigmoid(f_row)

    tt = lax.broadcasted_iota(jnp.int32, (chunk, chunk), 0)
    ss = lax.broadcasted_iota(jnp.int32, (chunk, chunk), 1)
    causal = (ss >= tt) if reverse else (ss <= tt)
    anti = (tt >= ss) if reverse else (tt <= ss)
    b_col = jnp.sum(jnp.where(causal, lf_row, 0.0), axis=1, keepdims=True)
    b_row = jnp.sum(jnp.where(anti, lf_col, 0.0), axis=0, keepdims=True)
    g_tot = jnp.sum(lf_row, axis=1, keepdims=True)

    m_old = m_scr[:, 0:1]
    a_row = g_tot - b_row + i_row
    a_col = g_tot - b_col + i_col
    m_new = jnp.maximum(g_tot + m_old, jnp.max(a_row, axis=1, keepdims=True))
    wk_col = jnp.exp(a_col - m_new)
    decay = jnp.exp(g_tot + m_old - m_new)

    q = q_ref[...]
    k = k_ref[...]
    v_aug = jnp.concatenate([v_ref[...], jnp.ones((chunk, LANES), BF16)], axis=1)
    c_old = c_scr[...]

    inter = jnp.dot(q, c_old.astype(BF16), preferred_element_type=F32) * qscale
    qk = lax.dot_general(q, k, _NT, preferred_element_type=F32) * qscale
    dmat = jnp.where(causal, b_col - b_row + i_row, -jnp.inf)
    m_inter = b_col + m_old
    m_t = jnp.maximum(m_inter, jnp.max(dmat, axis=1, keepdims=True))
    w_inter = jnp.exp(m_inter - m_t)
    pw = jnp.exp(dmat - m_t) * qk
    num = w_inter * inter + jnp.dot(pw.astype(BF16), v_aug, preferred_element_type=F32)
    den = num[:, dv:dv + 1]
    hc = num[:, :dv] / jnp.maximum(jnp.abs(den), jnp.exp(-m_t))

    kw = (k.astype(F32) * wk_col).astype(BF16)
    c_scr[...] = decay * c_old + lax.dot_general(kw, v_aug, _TN, preferred_element_type=F32)
    m_scr[...] = jnp.broadcast_to(m_new, m_scr.shape)

    if fuse_out:
        hs = hc + hb_ref[...]
        hn = hs * lax.rsqrt(jnp.mean(hs * hs, axis=-1, keepdims=True) + NORM_EPS) * nw_ref[...]
        h_ref[...] = (hn * jax.nn.sigmoid(og_ref[...].astype(F32))).astype(h_ref.dtype)
    else:
        h_ref[...] = hc

    @pl.when(step == nchunks - 1)
    def _():
        ct_ref[...] = c_scr[...]
        mt_ref[...] = m_scr[...]


def _mlstm_scan(proj, gates, state, n_tok, reverse, fuse=None):
    nh = MLSTM_HEADS
    m = proj.shape[0]
    dv = proj.shape[1] // (3 * nh)
    dqk = dv // 2
    batch = m // n_tok
    chunk = min(MLSTM_KERNEL_CHUNK, n_tok)
    nchunks = n_tok // chunk
    c0, m0 = state
    cidx = (lambda c: nchunks - 1 - c) if reverse else (lambda c: c)
    row = lambda b, c: b * nchunks + cidx(c)
    i_off, f_off = (2 * nh, 3 * nh) if reverse else (0, nh)
    in_specs = [pl.BlockSpec((chunk, dqk), lambda b, h, c: (row(b, c), h)),
                pl.BlockSpec((chunk, dqk), lambda b, h, c: (row(b, c), nh + h)),
                pl.BlockSpec((chunk, dv), lambda b, h, c: (row(b, c), nh + h)),
                pl.BlockSpec((chunk, LANES), lambda b, h, c: (row(b, c), 0)),
                pl.BlockSpec((None, None, dqk, dv + LANES), lambda b, h, c: (b, h, 0, 0)),
                pl.BlockSpec((None, None, 1, LANES), lambda b, h, c: (b, h, 0, 0))]
    args = [proj, proj, proj, gates, c0, m0]
    if fuse is not None:
        h_other, norm_w = fuse
        in_specs += [pl.BlockSpec((chunk, dv), lambda b, h, c: (row(b, c), h)),
                     pl.BlockSpec((chunk, dv), lambda b, h, c: (row(b, c), 2 * nh + h)),
                     pl.BlockSpec((1, dv), lambda b, h, c: (0, h))]
        args += [h_other, proj, norm_w.reshape(1, nh * dv)]
    kern = functools.partial(_mlstm_kernel, chunk=chunk, dv=dv, nchunks=nchunks, reverse=reverse,
                             i_off=i_off, f_off=f_off, fuse_out=fuse is not None, qscale=dqk ** -0.5)
    return pl.pallas_call(
        kern, grid=(batch, nh, nchunks), in_specs=in_specs,
        out_specs=[pl.BlockSpec((chunk, dv), lambda b, h, c: (row(b, c), h)),
                   pl.BlockSpec((None, None, dqk, dv + LANES), lambda b, h, c: (b, h, 0, 0)),
                   pl.BlockSpec((None, None, 1, LANES), lambda b, h, c: (b, h, 0, 0))],
        out_shape=[jax.ShapeDtypeStruct((m, nh * dv), F32 if fuse is None else BF16),
                   jax.ShapeDtypeStruct(c0.shape, F32),
                   jax.ShapeDtypeStruct(m0.shape, F32)],
        scratch_shapes=[pltpu.VMEM((dqk, dv + LANES), F32), pltpu.VMEM((1, LANES), F32)],
        compiler_params=_params(("arbitrary", "arbitrary", "arbitrary")),
        name="mlstm_scan_bwd" if reverse else "mlstm_scan_fwd")(*args)


def _excl_prefix(mask_f, upper):
    n_exp, t = mask_f.shape
    off = jnp.zeros((n_exp, 1), F32)
    parts = []
    for j in range(t // LANES):
        blk = mask_f[:, j * LANES:(j + 1) * LANES]
        parts.append(jnp.dot(blk.astype(BF16), upper, preferred_element_type=F32) + off)
        off = off + jnp.sum(blk, axis=1, keepdims=True)
    return jnp.concatenate(parts, axis=1)


def _select_kernel(aff_ref, idx_ref, gate_ref, slot_scr, *, cap, n_tok):
    e = pl.program_id(1)
    n_exp, t = aff_ref.shape

    @pl.when(e == 0)
    def _():
        a = aff_ref[...]
        bits = pltpu.bitcast(a, jnp.int32)
        thr = jnp.zeros((n_exp, 1), jnp.int32)
        for bit in range(30, -1, -1):
            cand = thr | (1 << bit)
            cnt = jnp.sum(jnp.where(bits >= cand, 1.0, 0.0), axis=1, keepdims=True)
            thr = jnp.where(cnt >= cap, cand, thr)
        uu = lax.broadcasted_iota(jnp.int32, (LANES, LANES), 0)
        vv = lax.broadcasted_iota(jnp.int32, (LANES, LANES), 1)
        upper = jnp.where(uu < vv, 1.0, 0.0).astype(BF16)
        gt = jnp.where(bits > thr, 1.0, 0.0)
        eq = jnp.where(bits == thr, 1.0, 0.0)
        need = cap - jnp.sum(gt, axis=1, keepdims=True)
        sel = gt + eq * jnp.where(_excl_prefix(eq, upper) < need, 1.0, 0.0)
        slot_scr[...] = jnp.where(sel > 0.5, _excl_prefix(sel, upper), -1.0)

    a = aff_ref[pl.ds(e, 1), :]
    slot = slot_scr[pl.ds(e, 1), :]
    a_hi = a.astype(BF16).astype(F32)
    a_mid = (a - a_hi).astype(BF16).astype(F32)
    a_lo = a - a_hi - a_mid
    tpos = lax.broadcasted_iota(jnp.int32, (1, t), 1)
    t_hi = (tpos >> 6).astype(F32)
    t_lo = (tpos & 63).astype(F32)
    row = lax.broadcasted_iota(jnp.int32, (16, t), 0)
    lhs = jnp.where(row == 0, t_hi, jnp.where(row == 1, t_lo, jnp.where(row == 2, a_hi, jnp.where(
        row == 3, a_mid, jnp.where(row == 4, a_lo, 0.0))))).astype(BF16)
    s_iota = lax.broadcasted_iota(jnp.int32, (cap, t), 0).astype(F32)
    onehot = jnp.where(slot == s_iota, 1.0, 0.0).astype(BF16)
    r = lax.dot_general(lhs, onehot, _NT, preferred_element_type=F32)
    idx_ref[...] = (r[0:1] * 64.0 + r[1:2]).astype(jnp.int32) + pl.program_id(0) * n_tok
    gate_ref[...] = r[2:3] + r[3:4] + r[4:5]


def _select(aff, cap):
    batch, n_exp, n_tok = aff.shape
    kern = functools.partial(_select_kernel, cap=cap, n_tok=n_tok)
    out_spec = pl.BlockSpec((None, None, 1, cap), lambda b, e: (b, e, 0, 0))
    return pl.pallas_call(
        kern, grid=(batch, n_exp),
        in_specs=[pl.BlockSpec((None, n_exp, n_tok), lambda b, e: (b, 0, 0))],
        out_specs=[out_spec, out_spec],
        out_shape=[jax.ShapeDtypeStruct((batch, n_exp, 1, cap), jnp.int32),
                   jax.ShapeDtypeStruct((batch, n_exp, 1, cap), F32)],
        scratch_shapes=[pltpu.VMEM((n_exp, n_tok), F32)],
        compiler_params=_params(("arbitrary", "arbitrary")), name="moe_select")(aff)


def _moe_ffn_kernel(idx_ref, h_hbm, x_hbm, w1_ref, w3_ref, w2_ref, gate_ref, g2_ref, o_hbm,
                    h_buf, x_buf, sem_h, sem_x, sem_o, *, tc, n_steps):
    t = (pl.program_id(0) * pl.num_programs(1) + pl.program_id(1)) * pl.num_programs(2) + pl.program_id(2)

    def h_copy(step, s, row):
        return pltpu.make_async_copy(h_hbm.at[pl.ds(row, 1)], h_buf.at[step % 2, pl.ds(s, 1)], sem_h.at[step % 2])

    def x_copy(step, s, row):
        return pltpu.make_async_copy(x_hbm.at[pl.ds(row, 1)], x_buf.at[step % 3, pl.ds(s, 1)], sem_x.at[step % 3])

    def o_copy(step, s, row):
        return pltpu.make_async_copy(x_buf.at[step % 3, pl.ds(s, 1)], o_hbm.at[pl.ds(row, 1)], sem_o.at[step % 3])

    def wait_h(step):
        pltpu.make_async_copy(h_hbm.at[pl.ds(0, tc)], h_buf.at[step % 2], sem_h.at[step % 2]).wait()

    def wait_x(step):
        pltpu.make_async_copy(x_hbm.at[pl.ds(0, tc)], x_buf.at[step % 3], sem_x.at[step % 3]).wait()

    def wait_o(step):
        pltpu.make_async_copy(x_buf.at[step % 3], o_hbm.at[pl.ds(0, tc)], sem_o.at[step % 3]).wait()

    @pl.when(t == 0)
    def _():
        def body(s, carry):
            row = idx_ref[s]
            h_copy(0, s, row).start()
            x_copy(0, s, row).start()
            return carry
        lax.fori_loop(0, tc, body, 0, unroll=DMA_UNROLL)

    @pl.when(t >= 2)
    def _():
        wait_o(t - 2)
    wait_h(t)
    wait_x(t)

    nxt = (t + 1) % n_steps
    for s in range(tc):
        row = idx_ref[nxt * tc + s]
        h_copy(t + 1, s, row).start()
        x_copy(t + 1, s, row).start()

    x = h_buf[t % 2].astype(BF16)
    a = jnp.dot(x, w1_ref[...], preferred_element_type=F32)
    u = jnp.dot(x, w3_ref[...], preferred_element_type=F32)
    hmid = (a * jax.nn.sigmoid(a) * u).astype(BF16)
    y = jnp.dot(hmid, w2_ref[...], preferred_element_type=F32)
    ii = lax.broadcasted_iota(jnp.int32, (tc, tc), 0)
    jj = lax.broadcasted_iota(jnp.int32, (tc, tc), 1)
    gate_col = jnp.sum(jnp.where(ii == jj, gate_ref[...], 0.0), axis=1, keepdims=True)
    x_buf[t % 3] = x_buf[t % 3] + g2_ref[...] * (y * gate_col)
    for s in range(tc):
        o_copy(t, s, idx_ref[t * tc + s]).start()

    @pl.when(t == n_steps - 1)
    def _():
        if n_steps >= 2:
            wait_o(t - 1)
        wait_o(t)
        wait_h(t + 1)
        wait_x(t + 1)


def _moe_ffn(idx, gate, h2, x, w1, w3, w2, gate2):
    batch, n_exp, _, cap = idx.shape
    m, d = x.shape
    ff = w1.shape[-1]
    tc = min(MOE_ROW_TILE, cap)
    tiles = cap // tc
    assert cap % tc == 0 and batch * tiles - (tiles - 1) >= 3
    n_steps = n_exp * batch * tiles
    g_idx = (lambda e, b, j, i: (b, 0, 0)) if gate2.shape[0] > 1 else (lambda e, b, j, i: (0, 0, 0))
    kern = functools.partial(_moe_ffn_kernel, tc=tc, n_steps=n_steps)
    grid_spec = pltpu.PrefetchScalarGridSpec(
        num_scalar_prefetch=1, grid=(n_exp, batch, tiles),
        in_specs=[pl.BlockSpec(memory_space=pl.ANY),
                  pl.BlockSpec(memory_space=pl.ANY),
                  pl.BlockSpec((None, d, ff), lambda e, b, j, i: (e, 0, 0)),
                  pl.BlockSpec((None, d, ff), lambda e, b, j, i: (e, 0, 0)),
                  pl.BlockSpec((None, ff, d), lambda e, b, j, i: (e, 0, 0)),
                  pl.BlockSpec((None, None, 1, tc), lambda e, b, j, i: (b, e, 0, j)),
                  pl.BlockSpec((None, 1, d), g_idx)],
        out_specs=pl.BlockSpec(memory_space=pl.ANY),
        scratch_shapes=[pltpu.VMEM((2, tc, d), F32), pltpu.VMEM((3, tc, d), F32),
                        pltpu.SemaphoreType.DMA((2,)), pltpu.SemaphoreType.DMA((3,)), pltpu.SemaphoreType.DMA((3,))])
    idx_steps = jnp.transpose(idx.reshape(batch, n_exp, cap), (1, 0, 2)).reshape(-1)
    return pl.pallas_call(
        kern, grid_spec=grid_spec,
        out_shape=jax.ShapeDtypeStruct((m, d), F32),
        input_output_aliases={2: 0},
        compiler_params=_params(("arbitrary", "arbitrary", "arbitrary")), name="moe_ffn",
    )(idx_steps, h2, x, w1, w3, w2, gate, gate2)


def _moe(x, norm_w, shift, scale, gate2, router_t, w1, w3, w2, n_tok):
    n_exp = router_t.shape[0]
    cap = (CAPACITY_FACTOR * n_tok) // n_exp
    h2, aff = _norm_mod(x, norm_w, shift, scale, n_tok, router_t=router_t)
    idx, gate = _select(aff, cap)
    return _moe_ffn(idx, gate, h2, x, w1, w3, w2, gate2)


def kernel(x, c, ctx, c_ctx, ada_w, ada_b, norm1_w, norm2_w, ab_w_in, ab_w_out, na_rpb, win_sink,
           ml_w_in, ml_b_gates, ml_norm_w, ml_w_out, moe_router, moe_w1, moe_w3, moe_w2, final_norm_w):
    batch, n, d = x.shape
    lc = ctx.shape[1]
    depth = ada_w.shape[0]
    hd = HEAD_DIM
    n_heads = d // hd
    n_na = n_heads // 2
    n_q = n_heads - n_na
    n_kv = max(1, n_q // 4)
    nh = MLSTM_HEADS
    dv = d // nh
    dqk = dv // 2
    assert batch + 1 <= 8

    xl = x.reshape(batch * n, d)
    xc = ctx.reshape(batch * lc, d)
    cc = jnp.concatenate([c, c_ctx[None], jnp.zeros((8 - batch - 1, d), F32)], axis=0)
    mods = _ada_mods(cc, ada_w, ada_b)
    cos, sin = _rope_tables(n)

    for l in range(depth):
        need_ctx = l < depth - 1
        ml = mods[l].reshape(8, ADA_MODS, 1, d)
        sh1, sc1, g1, sh2, sc2, g2 = (ml[:batch, j] for j in range(ADA_MODS))
        ch1, cs1, cg1, ch2, cs2, cg2 = (ml[batch:batch + 1, j] for j in range(ADA_MODS))
        hl = _norm_mod(xl, norm1_w[l], sh1, sc1, n)
        hc = _norm_mod(xc, norm1_w[l], ch1, cs1, lc)
        if l % 2 == 0:
            e = l // 2
            qkv_l = _proj(hl, ab_w_in, e, ab_w_in.shape[2])
            qkv_c = _proj(hc, ab_w_in, e, ab_w_in.shape[2])
            bias = _na_bias_pairs(na_rpb[e])
            na_l = _na_attention(qkv_l, qkv_c, bias, batch, n, lc, n_na)
            win_l = _win_attention(win_sink[e], qkv_l, qkv_c, cos, sin, batch, n, lc, n_na, n_q, n_kv)
            xl = _proj_residual([na_l, win_l], ab_w_out, e, xl, g1, n, in_place=l > 0)
            if need_ctx:
                g = n_q // n_kv
                na_c = _ctx_attention(win_sink[e], qkv_c, batch, lc, n_na, 1, 0, n_na, 2 * n_na, False)
                win_c = _ctx_attention(win_sink[e], qkv_c, batch, lc, n_kv, g,
                                       (3 * n_na) // g, 3 * n_na + n_q, 3 * n_na + n_q + n_kv, True)
                xc = _proj_residual([na_c, win_c], ab_w_out, e, xc, cg1, lc, in_place=l > 0)
        else:
            o = l // 2
            main_w = 2 * nh * dqk + 2 * nh * dv
            n_gates = ml_w_in.shape[2] - main_w
            w_gate = jnp.pad(ml_w_in[o, :, main_w:], ((0, 0), (0, LANES - n_gates))).astype(BF16)
            b_gate = jnp.pad(ml_b_gates[o], (0, LANES - n_gates))
            pl_l, pl_c = _proj(hl, ml_w_in, o, main_w), _proj(hc, ml_w_in, o, main_w)
            gt_l = _gate_proj(hl, w_gate, b_gate)
            gt_c = _gate_proj(hc, w_gate, b_gate)
            state0 = (jnp.zeros((batch, nh, dqk, dv + LANES), F32), jnp.zeros((batch, nh, 1, LANES), F32))
            hcb, cb, mb = _mlstm_scan(pl_c, gt_c, state0, lc, True)
            hlb, _, _ = _mlstm_scan(pl_l, gt_l, (cb, mb), n, True)
            yc, cf, mf = _mlstm_scan(pl_c, gt_c, state0, lc, False, fuse=(hcb, ml_norm_w[o]))
            yl, _, _ = _mlstm_scan(pl_l, gt_l, (cf, mf), n, False, fuse=(hlb, ml_norm_w[o]))
            xl = _proj_residual([yl], ml_w_out, o, xl, g1, n, in_place=l > 0)
            if need_ctx:
                xc = _proj_residual([yc], ml_w_out, o, xc, cg1, lc, in_place=l > 0)
        router_t = moe_router[l].T.astype(BF16)
        w1, w3, w2 = moe_w1[l].astype(BF16), moe_w3[l].astype(BF16), moe_w2[l].astype(BF16)
        xl = _moe(xl, norm2_w[l], sh2, sc2, g2, router_t, w1, w3, w2, n)
        if need_ctx:
            xc = _moe(xc, norm2_w[l], ch2, cs2, cg2, router_t, w1, w3, w2, lc)

    zeros = jnp.zeros((1, 1, d), F32)
    out = _norm_mod(xl, final_norm_w, zeros, zeros, n, out_dtype=F32)
    return out.reshape(batch, n, d)
```

```python
import functools

import numpy as np
import jax
import jax.numpy as jnp
from jax import lax
from jax.experimental import pallas as pl
from jax.experimental.pallas import tpu as pltpu

GRID_W = 64
HEAD_DIM = 128
NA_ROWS = 8
NA_COLS = 16
NA_BLOCK_ROWS = 4
WINDOW = 128
WIN_BLOCK = 128
WIN_STEP_BLOCKS = 8
ROPE_BASE = 10000.0
MLSTM_HEADS = 8
MLSTM_KERNEL_CHUNK = 512
CAPACITY_FACTOR = 2
ADA_MODS = 6
NORM_EPS = 1e-6
NEG_INF = -1e30
LANES = 128
MOE_ROW_TILE = 256
DMA_UNROLL = 8
V7X_VMEM_LIMIT_BYTES = 56 * 1024 * 1024

F32 = jnp.float32
BF16 = jnp.bfloat16
_NT = (((1,), (1,)), ((), ()))
_TN = (((0,), (0,)), ((), ()))


def _pick(n, cands):
    for c in cands:
        if n % c == 0:
            return c
    raise ValueError(f"no tile in {cands} divides {n}")


def _params(sem):
    return pltpu.CompilerParams(dimension_semantics=sem, vmem_limit_bytes=V7X_VMEM_LIMIT_BYTES)


def _ada_kernel(cc_ref, w_ref, b_ref, o_ref):
    x = cc_ref[...]
    s = x * jax.nn.sigmoid(x)
    o_ref[...] = jnp.dot(s.astype(BF16), w_ref[...].astype(BF16), preferred_element_type=F32) + b_ref[...]


def _ada_mods(cc, ada_w, ada_b):
    depth, d, n = ada_w.shape
    tn = _pick(n, (512, 256, 128))
    return pl.pallas_call(
        _ada_kernel,
        grid=(depth, n // tn),
        in_specs=[pl.BlockSpec((8, d), lambda l, j: (0, 0)),
                  pl.BlockSpec((None, d, tn), lambda l, j: (l, 0, j)),
                  pl.BlockSpec((None, 1, tn), lambda l, j: (l, 0, j))],
        out_specs=pl.BlockSpec((None, 8, tn), lambda l, j: (l, 0, j)),
        out_shape=jax.ShapeDtypeStruct((depth, 8, n), F32),
        compiler_params=_params(("arbitrary", "arbitrary")),
        name="ada_mods",
    )(cc, ada_w, ada_b.reshape(depth, 1, n))


def _norm_body(x_ref, w_ref, sh_ref, sc_ref):
    x = x_ref[...]
    y = x * lax.rsqrt(jnp.mean(x * x, axis=-1, keepdims=True) + NORM_EPS) * w_ref[...]
    return y * (1.0 + sc_ref[...]) + sh_ref[...]


def _norm_kernel(x_ref, w_ref, sh_ref, sc_ref, o_ref):
    o_ref[...] = _norm_body(x_ref, w_ref, sh_ref, sc_ref).astype(o_ref.dtype)


def _norm_router_kernel(x_ref, w_ref, sh_ref, sc_ref, wr_ref, o_ref, aff_ref):
    h = _norm_body(x_ref, w_ref, sh_ref, sc_ref)
    o_ref[...] = h
    logits = lax.dot_general(wr_ref[...], h.astype(BF16), _NT, preferred_element_type=F32)
    e = jnp.exp(logits - jnp.max(logits, axis=0, keepdims=True))
    aff_ref[...] = e / jnp.sum(e, axis=0, keepdims=True)


def _norm_mod(x, w, shift, scale, n_tok, out_dtype=BF16, router_t=None):
    m, d = x.shape
    tm = _pick(n_tok, (512, 256, 128))
    tiles = n_tok // tm
    per_sample = shift.shape[0] > 1
    mod_idx = (lambda i: (i // tiles, 0, 0)) if per_sample else (lambda i: (0, 0, 0))
    in_specs = [pl.BlockSpec((tm, d), lambda i: (i, 0)),
                pl.BlockSpec((1, d), lambda i: (0, 0)),
                pl.BlockSpec((None, 1, d), mod_idx),
                pl.BlockSpec((None, 1, d), mod_idx)]
    args = [x, w.reshape(1, d), shift, scale]
    o_spec = pl.BlockSpec((tm, d), lambda i: (i, 0))
    if router_t is None:
        return pl.pallas_call(
            _norm_kernel, grid=(m // tm,), in_specs=in_specs, out_specs=o_spec,
            out_shape=jax.ShapeDtypeStruct((m, d), out_dtype),
            compiler_params=_params(("arbitrary",)), name="norm_mod")(*args)
    n_exp = router_t.shape[0]
    return pl.pallas_call(
        _norm_router_kernel, grid=(m // tm,),
        in_specs=in_specs + [pl.BlockSpec((n_exp, d), lambda i: (0, 0))],
        out_specs=[o_spec, pl.BlockSpec((None, n_exp, tm), lambda i: (i // tiles, 0, i % tiles))],
        out_shape=[jax.ShapeDtypeStruct((m, d), F32),
                   jax.ShapeDtypeStruct((m // n_tok, n_exp, n_tok), F32)],
        compiler_params=_params(("arbitrary",)), name="norm_mod_router")(*args, router_t)


def _cast_weight_once(w_ref, wb_scr):
    @pl.when(pl.program_id(1) == 0)
    def _():
        wb_scr[...] = w_ref[...].astype(BF16)


def _proj_kernel(a_ref, w_ref, o_ref, wb_scr):
    _cast_weight_once(w_ref, wb_scr)
    o_ref[...] = jnp.dot(a_ref[...], wb_scr[...], preferred_element_type=F32).astype(o_ref.dtype)


def _proj_res_kernel(*refs):
    *a_refs, w_ref, r_ref, g_ref, o_ref, wb_scr = refs
    _cast_weight_once(w_ref, wb_scr)
    acc, k0 = None, 0
    for a_ref in a_refs:
        k1 = k0 + a_ref.shape[1]
        part = jnp.dot(a_ref[...], wb_scr[k0:k1, :], preferred_element_type=F32)
        acc, k0 = part if acc is None else acc + part, k1
    o_ref[...] = r_ref[...] + g_ref[...] * acc


def _mm_bias_kernel(a_ref, w_ref, b_ref, o_ref):
    o_ref[...] = jnp.dot(a_ref[...], w_ref[...], preferred_element_type=F32) + b_ref[...]


def _proj(a, w_stack, layer, n_cols, out_dtype=BF16):
    m, k = a.shape
    tm = _pick(m, (1024, 512, 256, 128))
    tn = _pick(n_cols, (512, 256, 128))
    return pl.pallas_call(
        _proj_kernel, grid=(n_cols // tn, m // tm),
        in_specs=[pl.BlockSpec((tm, k), lambda j, i: (i, 0)),
                  pl.BlockSpec((None, k, tn), lambda j, i: (layer, 0, j))],
        out_specs=pl.BlockSpec((tm, tn), lambda j, i: (i, j)),
        out_shape=jax.ShapeDtypeStruct((m, n_cols), out_dtype),
        scratch_shapes=[pltpu.VMEM((k, tn), BF16)],
        compiler_params=_params(("arbitrary", "arbitrary")), name="proj")(a, w_stack)


def _proj_residual(a_parts, w_stack, layer, res, gate, n_tok, in_place=True):
    m = a_parts[0].shape[0]
    _, k, n = w_stack.shape
    assert sum(a.shape[1] for a in a_parts) == k
    tm = _pick(n_tok, (1024, 512, 256, 128))
    tn = _pick(n, (512, 256, 128))
    tiles = n_tok // tm
    g_idx = (lambda j, i: (i // tiles, 0, j)) if gate.shape[0] > 1 else (lambda j, i: (0, 0, j))
    return pl.pallas_call(
        _proj_res_kernel, grid=(n // tn, m // tm),
        in_specs=[pl.BlockSpec((tm, a.shape[1]), lambda j, i: (i, 0)) for a in a_parts]
                 + [pl.BlockSpec((None, k, tn), lambda j, i: (layer, 0, j)),
                    pl.BlockSpec((tm, tn), lambda j, i: (i, j)),
                    pl.BlockSpec((None, 1, tn), g_idx)],
        out_specs=pl.BlockSpec((tm, tn), lambda j, i: (i, j)),
        out_shape=jax.ShapeDtypeStruct((m, n), F32),
        scratch_shapes=[pltpu.VMEM((k, tn), BF16)],
        input_output_aliases={len(a_parts) + 1: 0} if in_place else {},
        compiler_params=_params(("arbitrary", "arbitrary")), name="proj_residual")(*a_parts, w_stack, res, gate)


def _gate_proj(a, w, bias):
    m, k = a.shape
    n = w.shape[1]
    tm = _pick(m, (1024, 512, 256, 128))
    return pl.pallas_call(
        _mm_bias_kernel, grid=(m // tm,),
        in_specs=[pl.BlockSpec((tm, k), lambda i: (i, 0)), pl.BlockSpec((k, n), lambda i: (0, 0)),
                  pl.BlockSpec((1, n), lambda i: (0, 0))],
        out_specs=pl.BlockSpec((tm, n), lambda i: (i, 0)),
        out_shape=jax.ShapeDtypeStruct((m, n), F32),
        compiler_params=_params(("arbitrary",)), name="gate_proj")(a, w, bias.reshape(1, n))


def _na_bias_pairs(rpb):
    cols = np.arange(GRID_W)
    cstart = np.clip(cols - NA_COLS // 2, 0, GRID_W - NA_COLS)
    valid = (cols[None, :] >= cstart[:, None]) & (cols[None, :] < cstart[:, None] + NA_COLS)
    dc = cols[None, :] - cols[:, None] + NA_COLS - 1
    shift = (np.arange(2 * NA_COLS - 1)[:, None, None] == dc[None]) & valid[None]
    t = jnp.einsum("hrd,dqk->hrqk", rpb.astype(F32), jnp.asarray(shift, F32), precision=lax.Precision.HIGHEST)
    t = jnp.where(valid[None, None], t, NEG_INF)
    t = jnp.pad(t, ((0, 0), (1, 1), (0, 0), (0, 0)), constant_values=NEG_INF)
    return jnp.concatenate([t[:, :-1], t[:, 1:]], axis=-1)


def _na_kernel(q_ref, k_ref, v_ref, kc_ref, vc_ref, bias_ref, o_ref, *, rb, rows, kh, scale):
    gw, br = GRID_W, NA_BLOCK_ROWS
    wr = kh + br
    r0 = pl.program_id(2) * rb
    sc = lax.dot_general(q_ref[...], kc_ref[...], _NT, preferred_element_type=F32) * scale
    mc = jnp.max(sc, axis=-1, keepdims=True)
    mx_blocks, den_blocks, o_blocks = [], [], []
    for bi in range(rb // br):
        r4 = r0 + bi * br
        start = jnp.clip(r4 - kh // 2, 0, rows - wr)
        q = q_ref[bi * br * gw:(bi + 1) * br * gw, :]
        win = pl.ds(pl.multiple_of(start * gw, gw), wr * gw)
        bias = jnp.concatenate(
            [jnp.concatenate([bias_ref[jnp.clip(start - (r4 + a) + NA_ROWS + 2 * p, 0, 2 * NA_ROWS - 1)]
                              for p in range(wr // 2)], axis=1) for a in range(br)], axis=0)
        q_row = r4 + lax.broadcasted_iota(jnp.int32, (br * gw, 1), 0) // gw
        k_row = start + lax.broadcasted_iota(jnp.int32, (1, wr * gw), 1) // gw
        off = k_row - jnp.clip(q_row - kh // 2, 0, rows - kh)
        s = lax.dot_general(q, k_ref[win, :], _NT, preferred_element_type=F32) * scale + bias
        s = jnp.where(off >= 0, jnp.where(off < kh, s, NEG_INF), NEG_INF)
        mx = jnp.maximum(jnp.max(s, axis=-1, keepdims=True), mc[bi * br * gw:(bi + 1) * br * gw])
        p = jnp.exp(s - mx)
        mx_blocks.append(mx)
        den_blocks.append(jnp.sum(p, axis=-1, keepdims=True))
        o_blocks.append(jnp.dot(p.astype(BF16), v_ref[win, :], preferred_element_type=F32))
    pc = jnp.exp(sc - jnp.concatenate(mx_blocks, axis=0))
    den = jnp.concatenate(den_blocks, axis=0) + jnp.sum(pc, axis=-1, keepdims=True)
    o = jnp.concatenate(o_blocks, axis=0) + jnp.dot(pc.astype(BF16), vc_ref[...], preferred_element_type=F32)
    o_ref[...] = (o / den).astype(o_ref.dtype)


def _na_attention(qkv_l, qkv_c, bias, batch, n, lc, n_heads):
    rows = n // GRID_W
    kh = min(NA_ROWS, rows)
    assert kh % 2 == 0 and NA_BLOCK_ROWS % 2 == 0 and rows >= kh + NA_BLOCK_ROWS
    rb = _pick(rows, (16 * NA_BLOCK_ROWS, 8 * NA_BLOCK_ROWS, 4 * NA_BLOCK_ROWS, 2 * NA_BLOCK_ROWS, NA_BLOCK_ROWS))
    nrb = rows // rb
    hd = HEAD_DIM
    kern = functools.partial(_na_kernel, rb=rb, rows=rows, kh=kh, scale=hd ** -0.5)
    return pl.pallas_call(
        kern, grid=(batch, n_heads, nrb),
        in_specs=[pl.BlockSpec((rb * GRID_W, hd), lambda b, h, r: (b * nrb + r, h)),
                  pl.BlockSpec((n, hd), lambda b, h, r: (b, n_heads + h)),
                  pl.BlockSpec((n, hd), lambda b, h, r: (b, 2 * n_heads + h)),
                  pl.BlockSpec((lc, hd), lambda b, h, r: (b, n_heads + h)),
                  pl.BlockSpec((lc, hd), lambda b, h, r: (b, 2 * n_heads + h)),
                  pl.BlockSpec((None, 2 * NA_ROWS, GRID_W, 2 * GRID_W), lambda b, h, r: (h, 0, 0, 0))],
        out_specs=pl.BlockSpec((rb * GRID_W, hd), lambda b, h, r: (b * nrb + r, h)),
        out_shape=jax.ShapeDtypeStruct((batch * n, n_heads * hd), BF16),
        compiler_params=_params(("arbitrary", "arbitrary", "arbitrary")), name="na_attention",
    )(qkv_l, qkv_l, qkv_l, qkv_c, qkv_c, bias)


def _rope_tables(n):
    nf = HEAD_DIM // 4
    t = jnp.arange(n)
    inv = ROPE_BASE ** (-jnp.arange(nf, dtype=F32) / nf)
    ang_r = (t // GRID_W).astype(F32)[:, None] * inv
    ang_c = (t % GRID_W).astype(F32)[:, None] * inv
    cos = jnp.concatenate([jnp.cos(ang_r), jnp.cos(ang_r), jnp.cos(ang_c), jnp.cos(ang_c)], axis=-1)
    sin = jnp.concatenate([-jnp.sin(ang_r), jnp.sin(ang_r), -jnp.sin(ang_c), jnp.sin(ang_c)], axis=-1)
    return cos, sin


def _rope(x, cos, sin_signed):
    nf = HEAD_DIM // 4
    lane = lax.broadcasted_iota(jnp.int32, x.shape, 1)
    first = (lane % (2 * nf)) < nf
    swapped = jnp.where(first, pltpu.roll(x, HEAD_DIM - nf, 1), pltpu.roll(x, nf, 1))
    return x * cos + swapped * sin_signed


def _sink_column(sink_ref, base, g, rows_per_head):
    return jnp.concatenate([jnp.full((rows_per_head, 1), sink_ref[base + j], F32) for j in range(g)], axis=0)


def _win_kernel(sink_ref, q_ref, k_ref, v_ref, kc_ref, vc_ref, cq_ref, sq_ref, ck_ref, sk_ref, o_ref, kr_scr,
                *, g, n, scale, blocks):
    @pl.when(pl.program_id(2) == 0)
    def _():
        kr_scr[...] = _rope(k_ref[...].astype(F32), ck_ref[...], sk_ref[...]).astype(BF16)

    for u in range(blocks):
        rows = pl.ds(u * WIN_BLOCK, WIN_BLOCK)
        _win_block(sink_ref, q_ref.at[rows], kr_scr, v_ref, kc_ref, vc_ref, cq_ref.at[rows], sq_ref.at[rows],
                   o_ref.at[rows], pl.program_id(2) * blocks + u, g=g, n=n, scale=scale)


def _win_block(sink_ref, q_ref, kr_ref, v_ref, kc_ref, vc_ref, cq_ref, sq_ref, o_ref, blk, *, g, n, scale):
    hd = HEAD_DIM
    kvh = pl.program_id(1)
    span = WIN_BLOCK + 2 * WINDOW
    s0 = pl.multiple_of(jnp.clip(blk * WIN_BLOCK - WINDOW, 0, n - span), LANES)
    cq = cq_ref[...]
    sq = sq_ref[...]
    q = q_ref[...]
    qs = jnp.concatenate([_rope(q[:, j * hd:(j + 1) * hd].astype(F32), cq, sq) for j in range(g)],
                         axis=0).astype(BF16)
    kw = kr_ref[pl.ds(s0, span), :]
    vw = v_ref[pl.ds(s0, span), :]
    s = lax.dot_general(qs, kw, _NT, preferred_element_type=F32) * scale
    qpos = blk * WIN_BLOCK + lax.broadcasted_iota(jnp.int32, (g * WIN_BLOCK, 1), 0) % WIN_BLOCK
    kpos = s0 + lax.broadcasted_iota(jnp.int32, (1, span), 1)
    s = jnp.where(jnp.abs(qpos - kpos) <= WINDOW, s, NEG_INF)
    sc = lax.dot_general(qs, kc_ref[...], _NT, preferred_element_type=F32) * scale
    sink = _sink_column(sink_ref, kvh * g, g, WIN_BLOCK)
    mx = jnp.maximum(jnp.maximum(jnp.max(s, axis=-1, keepdims=True), jnp.max(sc, axis=-1, keepdims=True)), sink)
    p = jnp.exp(s - mx)
    pc = jnp.exp(sc - mx)
    den = jnp.sum(p, axis=-1, keepdims=True) + jnp.sum(pc, axis=-1, keepdims=True) + jnp.exp(sink - mx)
    o = (jnp.dot(p.astype(BF16), vw, preferred_element_type=F32)
         + jnp.dot(pc.astype(BF16), vc_ref[...], preferred_element_type=F32)) / den
    o_ref[...] = jnp.concatenate([o[j * WIN_BLOCK:(j + 1) * WIN_BLOCK] for j in range(g)],
                                 axis=1).astype(o_ref.dtype)


def _win_attention(sink, qkv_l, qkv_c, cos, sin, batch, n, lc, n_na, n_q, n_kv):
    hd = HEAD_DIM
    g = n_q // n_kv
    blocks = _pick(n // WIN_BLOCK, (WIN_STEP_BLOCKS, 1))
    tq = WIN_BLOCK * blocks
    nb = n // tq
    q_blk0 = (3 * n_na * hd) // (g * hd)
    k_blk0 = 3 * n_na + n_q
    assert (3 * n_na) % g == 0
    kern = functools.partial(_win_kernel, g=g, n=n, scale=hd ** -0.5, blocks=blocks)
    full = lambda b, kv, i: (0, 0)
    return pl.pallas_call(
        kern, grid=(batch, n_kv, nb),
        in_specs=[pl.BlockSpec(memory_space=pltpu.SMEM),
                  pl.BlockSpec((tq, g * hd), lambda b, kv, i: (b * nb + i, q_blk0 + kv)),
                  pl.BlockSpec((n, hd), lambda b, kv, i: (b, k_blk0 + kv)),
                  pl.BlockSpec((n, hd), lambda b, kv, i: (b, k_blk0 + n_kv + kv)),
                  pl.BlockSpec((lc, hd), lambda b, kv, i: (b, k_blk0 + kv)),
                  pl.BlockSpec((lc, hd), lambda b, kv, i: (b, k_blk0 + n_kv + kv)),
                  pl.BlockSpec((tq, hd), lambda b, kv, i: (i, 0)),
                  pl.BlockSpec((tq, hd), lambda b, kv, i: (i, 0)),
                  pl.BlockSpec((n, hd), full),
                  pl.BlockSpec((n, hd), full)],
        out_specs=pl.BlockSpec((tq, g * hd), lambda b, kv, i: (b * nb + i, kv)),
        out_shape=jax.ShapeDtypeStruct((batch * n, n_q * hd), BF16),
        scratch_shapes=[pltpu.VMEM((n, hd), BF16)],
        compiler_params=_params(("arbitrary", "arbitrary", "arbitrary")), name="window_attention",
    )(sink, qkv_l, qkv_l, qkv_l, qkv_c, qkv_c, cos, sin, cos, sin)


def _ctx_attn_kernel(sink_ref, q_ref, k_ref, v_ref, o_ref, *, g, lc, use_sink, scale):
    hd = HEAD_DIM
    q = q_ref[...]
    qs = jnp.concatenate([q[:, j * hd:(j + 1) * hd] for j in range(g)], axis=0)
    s = lax.dot_general(qs, k_ref[...], _NT, preferred_element_type=F32) * scale
    mx = jnp.max(s, axis=-1, keepdims=True)
    if use_sink:
        sink = _sink_column(sink_ref, pl.program_id(1) * g, g, lc)
        mx = jnp.maximum(mx, sink)
    p = jnp.exp(s - mx)
    den = jnp.sum(p, axis=-1, keepdims=True)
    if use_sink:
        den = den + jnp.exp(sink - mx)
    o = jnp.dot(p.astype(BF16), v_ref[...], preferred_element_type=F32) / den
    o_ref[...] = jnp.concatenate([o[j * lc:(j + 1) * lc] for j in range(g)], axis=1).astype(o_ref.dtype)


def _ctx_attention(sink, qkv_c, batch, lc, n_groups, g, q_blk0, k_blk0, v_blk0, use_sink):
    hd = HEAD_DIM
    kern = functools.partial(_ctx_attn_kernel, g=g, lc=lc, use_sink=use_sink, scale=hd ** -0.5)
    return pl.pallas_call(
        kern, grid=(batch, n_groups),
        in_specs=[pl.BlockSpec(memory_space=pltpu.SMEM),
                  pl.BlockSpec((lc, g * hd), lambda b, h: (b, q_blk0 + h)),
                  pl.BlockSpec((lc, hd), lambda b, h: (b, k_blk0 + h)),
                  pl.BlockSpec((lc, hd), lambda b, h: (b, v_blk0 + h))],
        out_specs=pl.BlockSpec((lc, g * hd), lambda b, h: (b, h)),
        out_shape=jax.ShapeDtypeStruct((batch * lc, n_groups * g * hd), BF16),
        compiler_params=_params(("arbitrary", "arbitrary")), name="ctx_attention",
    )(sink, qkv_c, qkv_c, qkv_c)


def _log_sigmoid(x):
    return jnp.minimum(x, 0.0) - jnp.log(1.0 + jnp.exp(-jnp.abs(x)))


def _mlstm_kernel(*refs, chunk, dv, nchunks, reverse, i_off, f_off, fuse_out, qscale):
    if fuse_out:
        (q_ref, k_ref, v_ref, g_ref, c0_ref, m0_ref, hb_ref, og_ref, nw_ref,
         h_ref, ct_ref, mt_ref, c_scr, m_scr) = refs
    else:
        q_ref, k_ref, v_ref, g_ref, c0_ref, m0_ref, h_ref, ct_ref, mt_ref, c_scr, m_scr = refs
    head = pl.program_id(1)
    step = pl.program_id(2)

    @pl.when(step == 0)
    def _():
        c_scr[...] = c0_ref[...]
        m_scr[...] = m0_ref[...]

    gates = g_ref[...]
    lane = lax.broadcasted_iota(jnp.int32, gates.shape, 1)
    i_col = jnp.sum(jnp.where(lane == i_off + head, gates, 0.0), axis=1, keepdims=True)
    f_col = jnp.sum(jnp.where(lane == f_off + head, gates, 0.0), axis=1, keepdims=True)
    gates_t = gates.T
    sub = lax.broadcasted_iota(jnp.int32, gates_t.shape, 0)
    i_row = jnp.sum(jnp.where(sub == i_off + head, gates_t, 0.0), axis=0, keepdims=True)
    f_row = jnp.sum(jnp.where(sub == f_off + head, gates_t, 0.0), axis=0, keepdims=True)
    lf_col = _log_sigmoid(f_col)
    lf_row = _log_sigmoid(f_row)

    tt = lax.broadcasted_iota(jnp.int32, (chunk, chunk), 0)
    ss = lax.broadcasted_iota(jnp.int32, (chunk, chunk), 1)
    causal = (ss >= tt) if reverse else (ss <= tt)
    anti = (tt >= ss) if reverse else (tt <= ss)
    b_col = jnp.sum(jnp.where(causal, lf_row, 0.0), axis=1, keepdims=True)
    b_row = jnp.sum(jnp.where(anti, lf_col, 0.0), axis=0, keepdims=True)
    g_tot = jnp.sum(lf_row, axis=1, keepdims=True)

    m_old = m_scr[:, 0:1]
    a_row = g_tot - b_row + i_row
    a_col = g_tot - b_col + i_col
    m_new = jnp.maximum(g_tot + m_old, jnp.max(a_row, axis=1, keepdims=True))
    wk_col = jnp.exp(a_col - m_new)
    decay = jnp.exp(g_tot + m_old - m_new)

    q = q_ref[...]
    k = k_ref[...]
    v_aug = jnp.concatenate([v_ref[...], jnp.ones((chunk, LANES), BF16)], axis=1)
    c_old = c_scr[...]

    inter = jnp.dot(q, c_old.astype(BF16), preferred_element_type=F32) * qscale
    qk = lax.dot_general(q, k, _NT, preferred_element_type=F32) * qscale
    dmat = jnp.where(causal, b_col - b_row + i_row, -jnp.inf)
    m_inter = b_col + m_old
    m_t = jnp.maximum(m_inter, jnp.max(dmat, axis=1, keepdims=True))
    w_inter = jnp.exp(m_inter - m_t)
    pw = jnp.exp(dmat - m_t) * qk
    num = w_inter * inter + jnp.dot(pw.astype(BF16), v_aug, preferred_element_type=F32)
    den = num[:, dv:dv + 1]
    hc = num[:, :dv] / jnp.maximum(jnp.abs(den), jnp.exp(-m_t))

    kw = (k.astype(F32) * wk_col).astype(BF16)
    c_scr[...] = decay * c_old + lax.dot_general(kw, v_aug, _TN, preferred_element_type=F32)
    m_scr[...] = jnp.broadcast_to(m_new, m_scr.shape)

    if fuse_out:
        hs = hc + hb_ref[...]
        hn = hs * lax.rsqrt(jnp.mean(hs * hs, axis=-1, keepdims=True) + NORM_EPS) * nw_ref[...]
        h_ref[...] = (hn * jax.nn.sigmoid(og_ref[...].astype(F32))).astype(h_ref.dtype)
    else:
        h_ref[...] = hc

    @pl.when(step == nchunks - 1)
    def _():
        ct_ref[...] = c_scr[...]
        mt_ref[...] = m_scr[...]


def _mlstm_scan(proj, gates, state, n_tok, reverse, fuse=None):
    nh = MLSTM_HEADS
    m = proj.shape[0]
    dv = proj.shape[1] // (3 * nh)
    dqk = dv // 2
    batch = m // n_tok
    chunk = min(MLSTM_KERNEL_CHUNK, n_tok)
    nchunks = n_tok // chunk
    c0, m0 = state
    cidx = (lambda c: nchunks - 1 - c) if reverse else (lambda c: c)
    row = lambda b, c: b * nchunks + cidx(c)
    i_off, f_off = (2 * nh, 3 * nh) if reverse else (0, nh)
    in_specs = [pl.BlockSpec((chunk, dqk), lambda b, h, c: (row(b, c), h)),
                pl.BlockSpec((chunk, dqk), lambda b, h, c: (row(b, c), nh + h)),
                pl.BlockSpec((chunk, dv), lambda b, h, c: (row(b, c), nh + h)),
                pl.BlockSpec((chunk, LANES), lambda b, h, c: (row(b, c), 0)),
                pl.BlockSpec((None, None, dqk, dv + LANES), lambda b, h, c: (b, h, 0, 0)),
                pl.BlockSpec((None, None, 1, LANES), lambda b, h, c: (b, h, 0, 0))]
    args = [proj, proj, proj, gates, c0, m0]
    if fuse is not None:
        h_other, norm_w = fuse
        in_specs += [pl.BlockSpec((chunk, dv), lambda b, h, c: (row(b, c), h)),
                     pl.BlockSpec((chunk, dv), lambda b, h, c: (row(b, c), 2 * nh + h)),
                     pl.BlockSpec((1, dv), lambda b, h, c: (0, h))]
        args += [h_other, proj, norm_w.reshape(1, nh * dv)]
    kern = functools.partial(_mlstm_kernel, chunk=chunk, dv=dv, nchunks=nchunks, reverse=reverse,
                             i_off=i_off, f_off=f_off, fuse_out=fuse is not None, qscale=dqk ** -0.5)
    return pl.pallas_call(
        kern, grid=(batch, nh, nchunks), in_specs=in_specs,
        out_specs=[pl.BlockSpec((chunk, dv), lambda b, h, c: (row(b, c), h)),
                   pl.BlockSpec((None, None, dqk, dv + LANES), lambda b, h, c: (b, h, 0, 0)),
                   pl.BlockSpec((None, None, 1, LANES), lambda b, h, c: (b, h, 0, 0))],
        out_shape=[jax.ShapeDtypeStruct((m, nh * dv), F32 if fuse is None else BF16),
                   jax.ShapeDtypeStruct(c0.shape, F32),
                   jax.ShapeDtypeStruct(m0.shape, F32)],
        scratch_shapes=[pltpu.VMEM((dqk, dv + LANES), F32), pltpu.VMEM((1, LANES), F32)],
        compiler_params=_params(("arbitrary", "arbitrary", "arbitrary")),
        name="mlstm_scan_bwd" if reverse else "mlstm_scan_fwd")(*args)


def _excl_prefix(mask_f, upper):
    n_exp, t = mask_f.shape
    off = jnp.zeros((n_exp, 1), F32)
    parts = []
    for j in range(t // LANES):
        blk = mask_f[:, j * LANES:(j + 1) * LANES]
        parts.append(jnp.dot(blk.astype(BF16), upper, preferred_element_type=F32) + off)
        off = off + jnp.sum(blk, axis=1, keepdims=True)
    return jnp.concatenate(parts, axis=1)


def _select_kernel(aff_ref, idx_ref, gate_ref, slot_scr, *, cap, n_tok):
    e = pl.program_id(1)
    n_exp, t = aff_ref.shape

    @pl.when(e == 0)
    def _():
        a = aff_ref[...]
        bits = pltpu.bitcast(a, jnp.int32)
        thr = jnp.zeros((n_exp, 1), jnp.int32)
        for bit in range(30, -1, -1):
            cand = thr | (1 << bit)
            cnt = jnp.sum(jnp.where(bits >= cand, 1.0, 0.0), axis=1, keepdims=True)
            thr = jnp.where(cnt >= cap, cand, thr)
        uu = lax.broadcasted_iota(jnp.int32, (LANES, LANES), 0)
        vv = lax.broadcasted_iota(jnp.int32, (LANES, LANES), 1)
        upper = jnp.where(uu < vv, 1.0, 0.0).astype(BF16)
        gt = jnp.where(bits > thr, 1.0, 0.0)
        eq = jnp.where(bits == thr, 1.0, 0.0)
        need = cap - jnp.sum(gt, axis=1, keepdims=True)
        sel = gt + eq * jnp.where(_excl_prefix(eq, upper) < need, 1.0, 0.0)
        slot_scr[...] = jnp.where(sel > 0.5, _excl_prefix(sel, upper), -1.0)

    a = aff_ref[pl.ds(e, 1), :]
    slot = slot_scr[pl.ds(e, 1), :]
    a_hi = a.astype(BF16).astype(F32)
    a_mid = (a - a_hi).astype(BF16).astype(F32)
    a_lo = a - a_hi - a_mid
    tpos = lax.broadcasted_iota(jnp.int32, (1, t), 1)
    t_hi = (tpos >> 6).astype(F32)
    t_lo = (tpos & 63).astype(F32)
    row = lax.broadcasted_iota(jnp.int32, (16, t), 0)
    lhs = jnp.where(row == 0, t_hi, jnp.where(row == 1, t_lo, jnp.where(row == 2, a_hi, jnp.where(
        row == 3, a_mid, jnp.where(row == 4, a_lo, 0.0))))).astype(BF16)
    s_iota = lax.broadcasted_iota(jnp.int32, (cap, t), 0).astype(F32)
    onehot = jnp.where(slot == s_iota, 1.0, 0.0).astype(BF16)
    r = lax.dot_general(lhs, onehot, _NT, preferred_element_type=F32)
    idx_ref[...] = (r[0:1] * 64.0 + r[1:2]).astype(jnp.int32) + pl.program_id(0) * n_tok
    gate_ref[...] = r[2:3] + r[3:4] + r[4:5]


def _select(aff, cap):
    batch, n_exp, n_tok = aff.shape
    kern = functools.partial(_select_kernel, cap=cap, n_tok=n_tok)
    out_spec = pl.BlockSpec((None, None, 1, cap), lambda b, e: (b, e, 0, 0))
    return pl.pallas_call(
        kern, grid=(batch, n_exp),
        in_specs=[pl.BlockSpec((None, n_exp, n_tok), lambda b, e: (b, 0, 0))],
        out_specs=[out_spec, out_spec],
        out_shape=[jax.ShapeDtypeStruct((batch, n_exp, 1, cap), jnp.int32),
                   jax.ShapeDtypeStruct((batch, n_exp, 1, cap), F32)],
        scratch_shapes=[pltpu.VMEM((n_exp, n_tok), F32)],
        compiler_params=_params(("arbitrary", "arbitrary")), name="moe_select")(aff)


def _moe_ffn_kernel(idx_ref, h_hbm, x_hbm, w1_ref, w3_ref, w2_ref, gate_ref, g2_ref, o_hbm,
                    h_buf, x_buf, sem_h, sem_x, sem_o, *, tc, n_steps):
    t = (pl.program_id(0) * pl.num_programs(1) + pl.program_id(1)) * pl.num_programs(2) + pl.program_id(2)

    def h_copy(step, s, row):
        return pltpu.make_async_copy(h_hbm.at[pl.ds(row, 1)], h_buf.at[step % 2, pl.ds(s, 1)], sem_h.at[step % 2])

    def x_copy(step, s, row):
        return pltpu.make_async_copy(x_hbm.at[pl.ds(row, 1)], x_buf.at[step % 3, pl.ds(s, 1)], sem_x.at[step % 3])

    def o_copy(step, s, row):
        return pltpu.make_async_copy(x_buf.at[step % 3, pl.ds(s, 1)], o_hbm.at[pl.ds(row, 1)], sem_o.at[step % 3])

    def wait_h(step):
        pltpu.make_async_copy(h_hbm.at[pl.ds(0, tc)], h_buf.at[step % 2], sem_h.at[step % 2]).wait()

    def wait_x(step):
        pltpu.make_async_copy(x_hbm.at[pl.ds(0, tc)], x_buf.at[step % 3], sem_x.at[step % 3]).wait()

    def wait_o(step):
        pltpu.make_async_copy(x_buf.at[step % 3], o_hbm.at[pl.ds(0, tc)], sem_o.at[step % 3]).wait()

    @pl.when(t == 0)
    def _():
        def body(s, carry):
            row = idx_ref[s]
            h_copy(0, s, row).start()
            x_copy(0, s, row).start()
            return carry
        lax.fori_loop(0, tc, body, 0, unroll=DMA_UNROLL)

    @pl.when(t >= 2)
    def _():
        wait_o(t - 2)
    wait_h(t)
    wait_x(t)

    nxt = (t + 1) % n_steps
    for s in range(tc):
        row = idx_ref[nxt * tc + s]
        h_copy(t + 1, s, row).start()
        x_copy(t + 1, s, row).start()

    x = h_buf[t % 2].astype(BF16)
    a = jnp.dot(x, w1_ref[...], preferred_element_type=F32)
    u = jnp.dot(x, w3_ref[...], preferred_element_type=F32)
    hmid = (a * jax.nn.sigmoid(a) * u).astype(BF16)
    y = jnp.dot(hmid, w2_ref[...], preferred_element_type=F32)
    ii = lax.broadcasted_iota(jnp.int32, (tc, tc), 0)
    jj = lax.broadcasted_iota(jnp.int32, (tc, tc), 1)
    gate_col = jnp.sum(jnp.where(ii == jj, gate_ref[...], 0.0), axis=1, keepdims=True)
    x_buf[t % 3] = x_buf[t % 3] + g2_ref[...] * (y * gate_col)
    for s in range(tc):
        o_copy(t, s, idx_ref[t * tc + s]).start()

    @pl.when(t == n_steps - 1)
    def _():
        if n_steps >= 2:
            wait_o(t - 1)
        wait_o(t)
        wait_h(t + 1)
        wait_x(t + 1)


def _moe_ffn(idx, gate, h2, x, w1, w3, w2, gate2):
    batch, n_exp, _, cap = idx.shape
    m, d = x.shape
    ff = w1.shape[-1]
    tc = min(MOE_ROW_TILE, cap)
    tiles = cap // tc
    assert cap % tc == 0 and batch * tiles - (tiles - 1) >= 3
    n_steps = n_exp * batch * tiles
    g_idx = (lambda e, b, j, i: (b, 0, 0)) if gate2.shape[0] > 1 else (lambda e, b, j, i: (0, 0, 0))
    kern = functools.partial(_moe_ffn_kernel, tc=tc, n_steps=n_steps)
    grid_spec = pltpu.PrefetchScalarGridSpec(
        num_scalar_prefetch=1, grid=(n_exp, batch, tiles),
        in_specs=[pl.BlockSpec(memory_space=pl.ANY),
                  pl.BlockSpec(memory_space=pl.ANY),
                  pl.BlockSpec((None, d, ff), lambda e, b, j, i: (e, 0, 0)),
                  pl.BlockSpec((None, d, ff), lambda e, b, j, i: (e, 0, 0)),
                  pl.BlockSpec((None, ff, d), lambda e, b, j, i: (e, 0, 0)),
                  pl.BlockSpec((None, None, 1, tc), lambda e, b, j, i: (b, e, 0, j)),
                  pl.BlockSpec((None, 1, d), g_idx)],
        out_specs=pl.BlockSpec(memory_space=pl.ANY),
        scratch_shapes=[pltpu.VMEM((2, tc, d), F32), pltpu.VMEM((3, tc, d), F32),
                        pltpu.SemaphoreType.DMA((2,)), pltpu.SemaphoreType.DMA((3,)), pltpu.SemaphoreType.DMA((3,))])
    idx_steps = jnp.transpose(idx.reshape(batch, n_exp, cap), (1, 0, 2)).reshape(-1)
    return pl.pallas_call(
        kern, grid_spec=grid_spec,
        out_shape=jax.ShapeDtypeStruct((m, d), F32),
        input_output_aliases={2: 0},
        compiler_params=_params(("arbitrary", "arbitrary", "arbitrary")), name="moe_ffn",
    )(idx_steps, h2, x, w1, w3, w2, gate, gate2)


def _moe(x, norm_w, shift, scale, gate2, router_t, w1, w3, w2, n_tok):
    n_exp = router_t.shape[0]
    cap = (CAPACITY_FACTOR * n_tok) // n_exp
    h2, aff = _norm_mod(x, norm_w, shift, scale, n_tok, router_t=router_t)
    idx, gate = _select(aff, cap)
    return _moe_ffn(idx, gate, h2, x, w1, w3, w2, gate2)


def kernel(x, c, ctx, c_ctx, ada_w, ada_b, norm1_w, norm2_w, ab_w_in, ab_w_out, na_rpb, win_sink,
           ml_w_in, ml_b_gates, ml_norm_w, ml_w_out, moe_router, moe_w1, moe_w3, moe_w2, final_norm_w):
    batch, n, d = x.shape
    lc = ctx.shape[1]
    depth = ada_w.shape[0]
    hd = HEAD_DIM
    n_heads = d // hd
    n_na = n_heads // 2
    n_q = n_heads - n_na
    n_kv = max(1, n_q // 4)
    nh = MLSTM_HEADS
    dv = d // nh
    dqk = dv // 2
    assert batch + 1 <= 8

    xl = x.reshape(batch * n, d)
    xc = ctx.reshape(batch * lc, d)
    cc = jnp.concatenate([c, c_ctx[None], jnp.zeros((8 - batch - 1, d), F32)], axis=0)
    mods = _ada_mods(cc, ada_w, ada_b)
    cos, sin = _rope_tables(n)

    for l in range(depth):
        need_ctx = l < depth - 1
        ml = mods[l].reshape(8, ADA_MODS, 1, d)
        sh1, sc1, g1, sh2, sc2, g2 = (ml[:batch, j] for j in range(ADA_MODS))
        ch1, cs1, cg1, ch2, cs2, cg2 = (ml[batch:batch + 1, j] for j in range(ADA_MODS))
        hl = _norm_mod(xl, norm1_w[l], sh1, sc1, n)
        hc = _norm_mod(xc, norm1_w[l], ch1, cs1, lc)
        if l % 2 == 0:
            e = l // 2
            qkv_l = _proj(hl, ab_w_in, e, ab_w_in.shape[2])
            qkv_c = _proj(hc, ab_w_in, e, ab_w_in.shape[2])
            bias = _na_bias_pairs(na_rpb[e])
            na_l = _na_attention(qkv_l, qkv_c, bias, batch, n, lc, n_na)
            win_l = _win_attention(win_sink[e], qkv_l, qkv_c, cos, sin, batch, n, lc, n_na, n_q, n_kv)
            xl = _proj_residual([na_l, win_l], ab_w_out, e, xl, g1, n, in_place=l > 0)
            if need_ctx:
                g = n_q // n_kv
                na_c = _ctx_attention(win_sink[e], qkv_c, batch, lc, n_na, 1, 0, n_na, 2 * n_na, False)
                win_c = _ctx_attention(win_sink[e], qkv_c, batch, lc, n_kv, g,
                                       (3 * n_na) // g, 3 * n_na + n_q, 3 * n_na + n_q + n_kv, True)
                xc = _proj_residual([na_c, win_c], ab_w_out, e, xc, cg1, lc, in_place=l > 0)
        else:
            o = l // 2
            main_w = 2 * nh * dqk + 2 * nh * dv
            n_gates = ml_w_in.shape[2] - main_w
            w_gate = jnp.pad(ml_w_in[o, :, main_w:], ((0, 0), (0, LANES - n_gates))).astype(BF16)
            b_gate = jnp.pad(ml_b_gates[o], (0, LANES - n_gates))
            pl_l, pl_c = _proj(hl, ml_w_in, o, main_w), _proj(hc, ml_w_in, o, main_w)
            gt_l = _gate_proj(hl, w_gate, b_gate)
            gt_c = _gate_proj(hc, w_gate, b_gate)
            state0 = (jnp.zeros((batch, nh, dqk, dv + LANES), F32), jnp.zeros((batch, nh, 1, LANES), F32))
            hcb, cb, mb = _mlstm_scan(pl_c, gt_c, state0, lc, True)
            hlb, _, _ = _mlstm_scan(pl_l, gt_l, (cb, mb), n, True)
            yc, cf, mf = _mlstm_scan(pl_c, gt_c, state0, lc, False, fuse=(hcb, ml_norm_w[o]))
            yl, _, _ = _mlstm_scan(pl_l, gt_l, (cf, mf), n, False, fuse=(hlb, ml_norm_w[o]))
            xl = _proj_residual([yl], ml_w_out, o, xl, g1, n, in_place=l > 0)
            if need_ctx:
                xc = _proj_residual([yc], ml_w_out, o, xc, cg1, lc, in_place=l > 0)
        router_t = moe_router[l].T.astype(BF16)
        w1, w3, w2 = moe_w1[l].astype(BF16), moe_w3[l].astype(BF16), moe_w2[l].astype(BF16)
        xl = _moe(xl, norm2_w[l], sh2, sc2, g2, router_t, w1, w3, w2, n)
        if need_ctx:
            xc = _moe(xc, norm2_w[l], ch2, cs2, cg2, router_t, w1, w3, w2, lc)

    zeros = jnp.zeros((1, 1, d), F32)
    out = _norm_mod(xl, final_norm_w, zeros, zeros, n, out_dtype=F32)
    return out.reshape(batch, n, d)
```
